```python
import math
import jax, jax.numpy as jnp
from jax import lax
import numpy as np

D_MODEL = 1024
BATCH = 32
SEQ = 2048
DEPTH = 2

GRID_W = 64
CTX_LEN = 256
CONV_W = 3
M_HEADS = 4
M_HEAD_DIM = 128
M_WIDTH = M_HEADS * M_HEAD_DIM
M_CHUNK = 64
A_HEADS = 4
A_SUB_DIM = 64
A_V_DIM = 2 * A_SUB_DIM
A_WIDTH = A_HEADS * A_V_DIM
Q_BLOCK = 128
ROPE_BASE = 10000.0
ROPE_FREQS = A_SUB_DIM // 4
S_HEADS = 8
S_HEAD_DIM = 64
S_WIDTH = S_HEADS * S_HEAD_DIM
S_GROUPS = 2
S_STATE = 128
S_CHUNK = 64
S_CONV_DIM = S_WIDTH + 2 * S_GROUPS * S_STATE
D_FF = 2816
N_BRANCH = 3
N_MOD = 6
IN_SPLITS = (2 * M_WIDTH, M_WIDTH, M_WIDTH, 4 * M_HEADS, A_WIDTH, A_WIDTH, A_WIDTH, S_WIDTH, S_CONV_DIM, 2 * S_HEADS, N_BRANCH * D_MODEL)
D_IN = sum(IN_SPLITS)
NORM_EPS = 1e-6

kernel_name = "hybrid_mlstm_diffattn_ssd_convffn_prefix"


def rmsnorm(x, g):
    xf = x.astype(jnp.float32)
    y = xf * lax.rsqrt(jnp.mean(xf * xf, axis=-1, keepdims=True) + NORM_EPS)
    return (y * g.astype(jnp.float32)).astype(x.dtype)


def modulate(h, shift, scale):
    return h * (1.0 + scale) + shift


def dwconv(x, w, b=None):
    k = w.shape[0]
    y = lax.conv_general_dilated(x, w[:, None, :].astype(x.dtype), window_strides=(1,),
                                 padding=[(k // 2, k // 2)], dimension_numbers=('NWC', 'WIO', 'NWC'),
                                 feature_group_count=x.shape[-1])
    if b is not None:
        y = y + b.astype(x.dtype)
    return y


def to_chunks(a, axis, size):
    shp = a.shape
    a = a.reshape(shp[:axis] + (shp[axis] // size, size) + shp[axis + 1:])
    return jnp.moveaxis(a, axis, 0)


def from_chunks(a, axis):
    a = jnp.moveaxis(a, 0, axis)
    shp = a.shape
    return a.reshape(shp[:axis] + (shp[axis] * shp[axis + 1],) + shp[axis + 2:])


def rev(a, axis, on):
    return jnp.flip(a, axis=axis) if on else a


def mlstm_scan(q, k, v, ig, lf, state, emit):
    L = M_CHUNK
    causal = jnp.tril(jnp.ones((L, L), dtype=bool))

    def step(carry, inp):
        C, n, m = carry
        qc, kc, vc, ic, fc = inp
        b = jnp.cumsum(fc, axis=-1)
        g_prev = b + m[..., None]
        d_mat = jnp.where(causal, b[..., :, None] - b[..., None, :] + ic[..., None, :], -jnp.inf)
        m_t = jnp.maximum(g_prev, jnp.max(d_mat, axis=-1))
        m_new = m_t[..., -1]
        w_s = jnp.exp(b[..., -1:] - b + ic - m_new[..., None])
        decay = jnp.exp(b[..., -1] + m - m_new)
        C_new = decay[..., None, None] * C + jnp.einsum('bhs,bhsk,bhsv->bhkv', w_s, kc, vc)
        n_new = decay[..., None] * n + jnp.einsum('bhs,bhsk->bhk', w_s, kc)
        if not emit:
            return (C_new, n_new, m_new), None
        w_prev = jnp.exp(g_prev - m_t)
        w = jnp.exp(d_mat - m_t[..., None]) * jnp.einsum('bhtk,bhsk->bhts', qc, kc)
        num = w_prev[..., None] * jnp.einsum('bhtk,bhkv->bhtv', qc, C) + jnp.einsum('bhts,bhsv->bhtv', w, vc)
        den = w_prev * jnp.einsum('bhtk,bhk->bht', qc, n) + jnp.sum(w, axis=-1)
        h = num / jnp.maximum(jnp.abs(den), jnp.exp(-m_t))[..., None]
        return (C_new, n_new, m_new), h

    xs = (to_chunks(q, 2, L), to_chunks(k, 2, L), to_chunks(v, 2, L), to_chunks(ig, 2, L), to_chunks(lf, 2, L))
    state, h = lax.scan(step, state, xs)
    return (from_chunks(h, 2) if emit else None), state


def mlstm_prep(qk_pre, v_pre, gate_pre, conv_w, conv_b, ig_b, fg_b):
    bsz, t = v_pre.shape[:2]
    qk = jax.nn.silu(dwconv(qk_pre, conv_w, conv_b)).astype(jnp.float32)
    q, k = jnp.split(qk, 2, axis=-1)

    def heads(a):
        return a.astype(jnp.float32).reshape(bsz, t, M_HEADS, M_HEAD_DIM).transpose(0, 2, 1, 3)

    g = gate_pre.astype(jnp.float32).reshape(bsz, t, 2, 2, M_HEADS)
    ig = (g[:, :, :, 0] + ig_b).transpose(2, 0, 3, 1)
    lf = jax.nn.log_sigmoid(g[:, :, :, 1] + fg_b).transpose(2, 0, 3, 1)
    return heads(q), heads(k) * M_HEAD_DIM ** -0.5, heads(v_pre), ig, lf


def mlstm_out(h, o_pre, norm_g):
    bsz, t = o_pre.shape[:2]
    h = rmsnorm(h.transpose(0, 2, 1, 3), norm_g.reshape(M_HEADS, M_HEAD_DIM)).reshape(bsz, t, M_WIDTH)
    return (jax.nn.sigmoid(o_pre.astype(jnp.float32)) * h).astype(o_pre.dtype)


def mlstm_mixer(lat, ctx, conv_w, conv_b, ig_b, fg_b, norm_g, emit_ctx):
    q, k, v, ig, lf = mlstm_prep(lat[0], lat[1], lat[3], conv_w, conv_b, ig_b, fg_b)
    qc, kc, vc, igc, lfc = mlstm_prep(ctx[0], ctx[1], ctx[3], conv_w, conv_b, ig_b, fg_b)
    bsz = q.shape[0]
    zeros = (jnp.zeros((bsz, M_HEADS, M_HEAD_DIM, M_HEAD_DIM), jnp.float32),
             jnp.zeros((bsz, M_HEADS, M_HEAD_DIM), jnp.float32),
             jnp.zeros((bsz, M_HEADS), jnp.float32))
    h_lat, h_ctx = 0.0, 0.0
    for d in range(2):
        hc, st = mlstm_scan(rev(qc, 2, d), rev(kc, 2, d), rev(vc, 2, d), rev(igc[d], 2, d), rev(lfc[d], 2, d), zeros, emit_ctx)
        hl, _ = mlstm_scan(rev(q, 2, d), rev(k, 2, d), rev(v, 2, d), rev(ig[d], 2, d), rev(lf[d], 2, d), st, True)
        h_lat = h_lat + rev(hl, 2, d)
        if emit_ctx:
            h_ctx = h_ctx + rev(hc, 2, d)
    out_l = mlstm_out(h_lat, lat[2], norm_g)
    out_c = mlstm_out(h_ctx, ctx[2], norm_g) if emit_ctx else None
    return out_l, out_c


def rope_2d(x, cos, sin):
    shp = x.shape
    xr = x.reshape(shp[:-1] + (2, 2, ROPE_FREQS))
    x1, x2 = xr[..., 0, :], xr[..., 1, :]
    cb, sb = cos[None, :, None, None], sin[None, :, None, None]
    return jnp.stack([x1 * cb - x2 * sb, x2 * cb + x1 * sb], axis=-2).reshape(shp)


def diff_attention(q, k, v, qc, kc, vc, qn_g, kn_g, lam_p, subln_g, lam_init, cos, sin, emit_ctx):
    def heads_qk(a, g, rot):
        bsz, t = a.shape[:2]
        a = rmsnorm(a.reshape(bsz, t, A_HEADS, 2, A_SUB_DIM), g).astype(jnp.float32)
        if rot:
            a = rope_2d(a, cos, sin)
        return a.transpose(0, 2, 3, 1, 4)

    def heads_v(a):
        bsz, t = a.shape[:2]
        return a.reshape(bsz, t, A_HEADS, A_V_DIM).transpose(0, 2, 1, 3).astype(jnp.float32)

    lam_p = lam_p.astype(jnp.float32)
    lam = jnp.exp(jnp.sum(lam_p[0] * lam_p[1])) - jnp.exp(jnp.sum(lam_p[2] * lam_p[3])) + lam_init
    scale = A_SUB_DIM ** -0.5

    def attend(qb, kk, vv):
        s = jnp.einsum('bhjqd,bhjkd->bhjqk', qb, kk) * scale
        p = jax.nn.softmax(s, axis=-1)
        return jnp.einsum('bhqk,bhkv->bhqv', p[:, :, 0] - lam * p[:, :, 1], vv)

    k_ctx, v_ctx = heads_qk(kc, kn_g, False), heads_v(vc)
    k_all = jnp.concatenate([heads_qk(k, kn_g, True), k_ctx], axis=3)
    v_all = jnp.concatenate([heads_v(v), v_ctx], axis=2)
    q_blocks = to_chunks(heads_qk(q, qn_g, True), 3, Q_BLOCK)
    o_lat = from_chunks(lax.map(lambda qb: attend(qb, k_all, v_all), q_blocks), 2)

    def post(o):
        o = rmsnorm(o.transpose(0, 2, 1, 3), subln_g) * (1.0 - lam_init)
        return o.reshape(o.shape[0], o.shape[1], A_WIDTH).astype(q.dtype)

    out_c = post(attend(heads_qk(qc, qn_g, False), k_ctx, v_ctx)) if emit_ctx else None
    return post(o_lat), out_c


def ssd_scan(x, dt, a_coef, bm, cm, state, emit):
    L = S_CHUNK
    rep = S_HEADS // S_GROUPS
    causal = jnp.tril(jnp.ones((L, L), dtype=bool))[None, :, :, None]

    def step(h, inp):
        xk, dtk, bk, ck = inp
        s = jnp.cumsum(dtk * a_coef, axis=1)
        bh = jnp.repeat(bk, rep, axis=2)
        xdt = xk * dtk[..., None]
        h_new = (jnp.exp(s[:, -1])[..., None, None] * h
                 + jnp.einsum('bsh,bshp,bshn->bhpn', jnp.exp(s[:, -1:] - s), xdt, bh))
        if not emit:
            return h_new, None
        ch = jnp.repeat(ck, rep, axis=2)
        seg = jnp.exp(jnp.where(causal, s[:, :, None] - s[:, None], -jnp.inf))
        y = (jnp.einsum('btsh,bshp->bthp', jnp.einsum('bthn,bshn->btsh', ch, bh) * seg, xdt)
             + jnp.exp(s)[..., None] * jnp.einsum('bthn,bhpn->bthp', ch, h))
        return h_new, y

    xs = (to_chunks(x, 1, L), to_chunks(dt, 1, L), to_chunks(bm, 1, L), to_chunks(cm, 1, L))
    state, y = lax.scan(step, state, xs)
    return (from_chunks(y, 1) if emit else None), state


def ssd_prep(xbc_pre, dt_pre, conv_w, conv_b, dt_bias):
    bsz, t = xbc_pre.shape[:2]
    xbc = jax.nn.silu(dwconv(xbc_pre, conv_w, conv_b)).astype(jnp.float32)
    xs, bm, cm = jnp.split(xbc, [S_WIDTH, S_WIDTH + S_GROUPS * S_STATE], axis=-1)
    dt = jax.nn.softplus(dt_pre.astype(jnp.float32).reshape(bsz, t, 2, S_HEADS) + dt_bias)
    return (xs.reshape(bsz, t, S_HEADS, S_HEAD_DIM), bm.reshape(bsz, t, S_GROUPS, S_STATE),
            cm.reshape(bsz, t, S_GROUPS, S_STATE), dt)


def ssd_out(y, xs, z, d_skip, norm_g):
    bsz, t = z.shape[:2]
    y = (y + d_skip.astype(jnp.float32)[:, None] * xs).reshape(bsz, t, S_WIDTH) * jax.nn.silu(z.astype(jnp.float32))
    y = rmsnorm(y.reshape(bsz, t, S_GROUPS, S_WIDTH // S_GROUPS), norm_g.reshape(S_GROUPS, S_WIDTH // S_GROUPS))
    return y.reshape(bsz, t, S_WIDTH).astype(z.dtype)


def ssd_mixer(z, xbc_pre, dt_pre, zc, xbcc_pre, dtc_pre, conv_w, conv_b, dt_bias, a_log, d_skip, norm_g, emit_ctx):
    a_coef = -jnp.exp(a_log.astype(jnp.float32))
    x, bm, cm, dt = ssd_prep(xbc_pre, dt_pre, conv_w, conv_b, dt_bias)
    xc, bc, cc, dtc = ssd_prep(xbcc_pre, dtc_pre, conv_w, conv_b, dt_bias)
    zeros = jnp.zeros((x.shape[0], S_HEADS, S_HEAD_DIM, S_STATE), jnp.float32)
    y_lat, y_ctx = 0.0, 0.0
    for d in range(2):
        yc, st = ssd_scan(rev(xc, 1, d), rev(dtc[:, :, d], 1, d), a_coef[d], rev(bc, 1, d), rev(cc, 1, d), zeros, emit_ctx)
        yl, _ = ssd_scan(rev(x, 1, d), rev(dt[:, :, d], 1, d), a_coef[d], rev(bm, 1, d), rev(cm, 1, d), st, True)
        y_lat = y_lat + rev(yl, 1, d)
        if emit_ctx:
            y_ctx = y_ctx + rev(yc, 1, d)
    out_l = ssd_out(y_lat, x, z, d_skip, norm_g)
    out_c = ssd_out(y_ctx, xc, zc, d_skip, norm_g) if emit_ctx else None
    return out_l, out_c


def gated_merge(y_m, y_a, y_s, gate_pre, w_m, w_a, w_s, w_o):
    g_m, g_a, g_s = jnp.split(jax.nn.sigmoid(gate_pre), N_BRANCH, axis=-1)
    return (g_m * (y_m @ w_m) + g_a * (y_a @ w_a) + g_s * (y_s @ w_s)) @ w_o


def conv_ffn(h, w_up, conv_w, w_down):
    a, g = jnp.split(dwconv(h @ w_up, conv_w), 2, axis=-1)
    return (jax.nn.silu(g) * a) @ w_down


def setup_inputs(seed: int = 0) -> dict:
    key = jax.random.key(seed)
    ks = jax.random.split(key, 32)
    f32 = jnp.float32

    def nrm(i, shape, scale):
        return scale * jax.random.normal(ks[i], shape, f32)

    def gain(i, shape):
        return 1.0 + nrm(i, shape, 0.05)

    dt0 = jnp.exp(jax.random.uniform(ks[20], (DEPTH, 2, S_HEADS), f32, math.log(1e-3), math.log(1e-1)))
    return {
        "x": nrm(0, (BATCH, SEQ, D_MODEL), 1.0),
        "c": nrm(1, (BATCH, D_MODEL), 1.0),
        "ctx": nrm(2, (BATCH, CTX_LEN, D_MODEL), 1.0),
        "c_ctx": nrm(3, (D_MODEL,), 1.0),
        "w_mod": nrm(4, (DEPTH, D_MODEL, N_MOD * D_MODEL), 0.5 * D_MODEL ** -0.5),
        "b_mod": nrm(5, (DEPTH, N_MOD * D_MODEL), 0.02),
        "norm1_g": gain(6, (DEPTH, D_MODEL)),
        "norm2_g": gain(7, (DEPTH, D_MODEL)),
        "w_in": nrm(8, (DEPTH, D_MODEL, D_IN), D_MODEL ** -0.5),
        "m_conv_w": nrm(9, (DEPTH, CONV_W, 2 * M_WIDTH), CONV_W ** -0.5),
        "m_conv_b": nrm(10, (DEPTH, 2 * M_WIDTH), 0.02),
        "m_igate_b": nrm(11, (DEPTH, 2, M_HEADS), 0.1),
        "m_fgate_b": jnp.linspace(3.0, 6.0, M_HEADS, dtype=f32) + nrm(12, (DEPTH, 2, M_HEADS), 0.1),
        "m_norm_g": gain(13, (DEPTH, M_WIDTH)),
        "a_qnorm_g": gain(14, (DEPTH, A_SUB_DIM)),
        "a_knorm_g": gain(15, (DEPTH, A_SUB_DIM)),
        "a_lambda": nrm(16, (DEPTH, 4, A_SUB_DIM), 0.1),
        "a_subln_g": gain(17, (DEPTH, A_V_DIM)),
        "s_conv_w": nrm(18, (DEPTH, CONV_W, S_CONV_DIM), CONV_W ** -0.5),
        "s_conv_b": nrm(19, (DEPTH, S_CONV_DIM), 0.02),
        "s_dt_bias": dt0 + jnp.log(-jnp.expm1(-dt0)),
        "s_a_log": jnp.log(jax.random.uniform(ks[21], (DEPTH, 2, S_HEADS), f32, 1.0, 16.0)),
        "s_d": gain(22, (DEPTH, S_HEADS)),
        "s_norm_g": gain(23, (DEPTH, S_WIDTH)),
        "w_branch_m": nrm(24, (DEPTH, M_WIDTH, D_MODEL), M_WIDTH ** -0.5),
        "w_branch_a": nrm(25, (DEPTH, A_WIDTH, D_MODEL), A_WIDTH ** -0.5),
        "w_branch_s": nrm(26, (DEPTH, S_WIDTH, D_MODEL), S_WIDTH ** -0.5),
        "w_out": nrm(27, (DEPTH, D_MODEL, D_MODEL), D_MODEL ** -0.5),
        "w_up": nrm(28, (DEPTH, D_MODEL, 2 * D_FF), D_MODEL ** -0.5),
        "ffn_conv_w": nrm(29, (DEPTH, CONV_W, 2 * D_FF), CONV_W ** -0.5),
        "w_down": nrm(30, (DEPTH, D_FF, D_MODEL), D_FF ** -0.5),
    }


def reference(x, c, ctx, c_ctx, w_mod, b_mod, norm1_g, norm2_g, w_in, m_conv_w, m_conv_b, m_igate_b, m_fgate_b,
              m_norm_g, a_qnorm_g, a_knorm_g, a_lambda, a_subln_g, s_conv_w, s_conv_b, s_dt_bias, s_a_log, s_d,
              s_norm_g, w_branch_m, w_branch_a, w_branch_s, w_out, w_up, ffn_conv_w, w_down):
    t = x.shape[1]
    rows = t // GRID_W
    row = jnp.repeat(jnp.arange(rows), GRID_W)
    col = jnp.tile(jnp.arange(GRID_W), rows)
    inv_freq = ROPE_BASE ** (-jnp.arange(ROPE_FREQS, dtype=jnp.float32) / ROPE_FREQS)
    ang = jnp.stack([row, col], axis=-1).astype(jnp.float32)[..., None] * inv_freq
    cos, sin = jnp.cos(ang), jnp.sin(ang)
    split_pts = [int(p) for p in np.cumsum(IN_SPLITS)[:-1]]

    xc = ctx
    for l in range(DEPTH):
        emit = l < DEPTH - 1
        lam_init = 0.8 - 0.6 * math.exp(-0.3 * l)
        mod = jnp.split(jax.nn.silu(c) @ w_mod[l] + b_mod[l], N_MOD, axis=-1)
        modc = jnp.split(jax.nn.silu(c_ctx) @ w_mod[l] + b_mod[l], N_MOD, axis=-1)

        h = modulate(rmsnorm(x, norm1_g[l]), mod[0][:, None], mod[1][:, None])
        hc = modulate(rmsnorm(xc, norm1_g[l]), modc[0], modc[1])
        u = jnp.split(h @ w_in[l], split_pts, axis=-1)
        uc = jnp.split(hc @ w_in[l], split_pts, axis=-1)
        y_m, y_mc = mlstm_mixer(u[0:4], uc[0:4], m_conv_w[l], m_conv_b[l], m_igate_b[l], m_fgate_b[l], m_norm_g[l], emit)
        y_a, y_ac = diff_attention(u[4], u[5], u[6], uc[4], uc[5], uc[6], a_qnorm_g[l], a_knorm_g[l], a_lambda[l],
                                   a_subln_g[l], lam_init, cos, sin, emit)
        y_s, y_sc = ssd_mixer(u[7], u[8], u[9], uc[7], uc[8], uc[9], s_conv_w[l], s_conv_b[l], s_dt_bias[l],
                              s_a_log[l], s_d[l], s_norm_g[l], emit)
        x = x + mod[2][:, None] * gated_merge(y_m, y_a, y_s, u[10], w_branch_m[l], w_branch_a[l], w_branch_s[l], w_out[l])

        h2 = modulate(rmsnorm(x, norm2_g[l]), mod[3][:, None], mod[4][:, None])
        x = x + mod[5][:, None] * conv_ffn(h2, w_up[l], ffn_conv_w[l], w_down[l])

        if emit:
            xc = xc + modc[2] * gated_merge(y_mc, y_ac, y_sc, uc[10], w_branch_m[l], w_branch_a[l], w_branch_s[l], w_out[l])
            hc2 = modulate(rmsnorm(xc, norm2_g[l]), modc[3], modc[4])
            xc = xc + modc[5] * conv_ffn(hc2, w_up[l], ffn_conv_w[l], w_down[l])
    return x
```

```python
import functools
import math

import jax
import jax.numpy as jnp
from jax import lax
from jax.experimental import pallas as pl
from jax.experimental.pallas import tpu as pltpu

F32 = jnp.float32
BF16 = jnp.bfloat16

D_MODEL = 1024
DEPTH = 2
GRID_W = 64
M_HEADS = 4
M_HEAD_DIM = 128
M_WIDTH = 512
A_HEADS = 4
A_SUB_DIM = 64
A_V_DIM = 128
A_WIDTH = 512
ROPE_BASE = 10000.0
ROPE_FREQS = 16
S_HEADS = 8
S_HEAD_DIM = 64
S_WIDTH = 512
S_GROUPS = 2
S_STATE = 128
CHUNK = 64
D_FF = 2816
N_MOD = 6
NORM_EPS = 1e-6
LANES = 128
U_MAIN = 8192
_MG0, _MG1 = 2048, 2064
_DT0, _DT1 = 5136, 5152
VMEM_LIMIT = 56 * 1024 * 1024


def _cparams(sem):
    return pltpu.CompilerParams(dimension_semantics=sem, vmem_limit_bytes=VMEM_LIMIT)


def _sigmoid(x):
    return 1.0 / (1.0 + jnp.exp(-x))


def _silu(x):
    return x * _sigmoid(x)


def _softplus(x):
    return jnp.maximum(x, 0.0) + jnp.log1p(jnp.exp(-jnp.abs(x)))


def _dot(a, b):
    return jnp.dot(a.astype(BF16), b.astype(BF16), preferred_element_type=F32)


def _dot_nt(a, b):
    return lax.dot_general(a.astype(BF16), b.astype(BF16), (((1,), (1,)), ((), ())),
                           preferred_element_type=F32)


def _dwconv3(x, w, b=None):
    t = x.shape[0]
    row = lax.broadcasted_iota(jnp.int32, x.shape, 0)
    prev = jnp.where(row == 0, 0.0, pltpu.roll(x, 1, 0))
    nxt = jnp.where(row == t - 1, 0.0, pltpu.roll(x, t - 1, 0))
    y = prev * w[0:1] + x * w[1:2] + nxt * w[2:3]
    if b is not None:
        y = y + b
    return y


def _cumsum_rows(x, reverse):
    n = x.shape[0]
    row = lax.broadcasted_iota(jnp.int32, x.shape, 0)
    k = 1
    while k < n:
        if reverse:
            x = x + jnp.where(row < n - k, pltpu.roll(x, n - k, 0), 0.0)
        else:
            x = x + jnp.where(row >= k, pltpu.roll(x, k, 0), 0.0)
        k *= 2
    return x


def _rms(x, g):
    return x * lax.rsqrt(jnp.mean(x * x, axis=-1, keepdims=True) + NORM_EPS) * g


def _mod_kernel(c_ref, w_ref, b_ref, o_ref):
    o_ref[0] = _dot(_silu(c_ref[...]), w_ref[0]) + b_ref[0]


def _modulation(c_all, w_mod, b_mod):
    rows = c_all.shape[0]
    tn = 1536
    return pl.pallas_call(
        _mod_kernel,
        grid=(DEPTH, N_MOD * D_MODEL // tn),
        in_specs=[pl.BlockSpec((rows, D_MODEL), lambda l, j: (0, 0)),
                  pl.BlockSpec((1, D_MODEL, tn), lambda l, j: (l, 0, j)),
                  pl.BlockSpec((1, 1, tn), lambda l, j: (l, 0, j))],
        out_specs=pl.BlockSpec((1, rows, tn), lambda l, j: (l, 0, j)),
        out_shape=jax.ShapeDtypeStruct((DEPTH, rows, N_MOD * D_MODEL), F32),
        compiler_params=_cparams(("arbitrary", "arbitrary")),
        name="modulation",
    )(c_all, w_mod.astype(BF16), b_mod[:, None, :])


def _inproj_kernel(x_ref, sh_ref, sc_ref, g_ref, w_ref, ws_ref, u_ref, ug_ref, h_scr):
    @pl.when(pl.program_id(2) == 0)
    def _():
        h = _rms(x_ref[0], g_ref[...]) * (1.0 + sc_ref[0]) + sh_ref[0]
        hb = h.astype(BF16)
        h_scr[...] = hb
        ug_ref[0] = jnp.dot(hb, ws_ref[...], preferred_element_type=F32)

    u_ref[0] = jnp.dot(h_scr[...], w_ref[...], preferred_element_type=F32).astype(u_ref.dtype)


def _inproj(x, shift, scale, g, w_main, w_small):
    b, t, _ = x.shape
    tm = min(t, 1024)
    tn = 2048
    per_batch = shift.shape[0] > 1
    mod_map = (lambda bi, i, j: (bi, 0, 0)) if per_batch else (lambda bi, i, j: (0, 0, 0))
    return pl.pallas_call(
        _inproj_kernel,
        grid=(b, t // tm, U_MAIN // tn),
        in_specs=[pl.BlockSpec((1, tm, D_MODEL), lambda bi, i, j: (bi, i, 0)),
                  pl.BlockSpec((1, 1, D_MODEL), mod_map),
                  pl.BlockSpec((1, 1, D_MODEL), mod_map),
                  pl.BlockSpec((1, D_MODEL), lambda bi, i, j: (0, 0)),
                  pl.BlockSpec((D_MODEL, tn), lambda bi, i, j: (0, j)),
                  pl.BlockSpec((D_MODEL, LANES), lambda bi, i, j: (0, 0))],
        out_specs=[pl.BlockSpec((1, tm, tn), lambda bi, i, j: (bi, i, j)),
                   pl.BlockSpec((1, tm, LANES), lambda bi, i, j: (bi, i, 0))],
        out_shape=[jax.ShapeDtypeStruct((b, t, U_MAIN), BF16),
                   jax.ShapeDtypeStruct((b, t, LANES), F32)],
        scratch_shapes=[pltpu.VMEM((tm, D_MODEL), BF16)],
        compiler_params=_cparams(("parallel", "parallel", "arbitrary")),
        name="inproj",
    )(x, shift, scale, g, w_main, w_small)


def _mlstm_stream(d, h, emit, q, k, v, gt, gtt, bt, btt, c_s, n_s, m_s):
    ci = d * 8 + h
    cf = d * 8 + 4 + h
    tl = CHUNK - 1 if d == 0 else 0
    icol, bcol = gt[:, ci:ci + 1], bt[:, cf:cf + 1]
    irow, brow = gtt[ci:ci + 1, :], btt[cf:cf + 1, :]
    r = lax.broadcasted_iota(jnp.int32, (CHUNK, CHUNK), 0)
    s = lax.broadcasted_iota(jnp.int32, (CHUNK, CHUNK), 1)
    mask = (s <= r) if d == 0 else (s >= r)
    c_old = c_s[d, h]
    n_old = n_s[d, h]
    m_old = m_s[d, h][:, 0:1]
    g_prev = bcol + m_old
    d_mat = jnp.where(mask, bcol + (irow - brow), -jnp.inf)
    m_t = jnp.maximum(g_prev, jnp.max(d_mat, axis=-1, keepdims=True))
    m_new = m_t[tl:tl + 1, :]
    b_last = bcol[tl:tl + 1, :]
    w_s = jnp.exp(b_last - bcol + icol - m_new)
    decay = jnp.exp(b_last + m_old - m_new)
    kf = k.astype(F32)
    kw = kf * w_s
    c_s[d, h] = decay * c_old + _dot(kw.T, v)
    n_s[d, h] = decay * n_old + jnp.sum(kw, axis=0, keepdims=True)
    m_s[d, h] = jnp.broadcast_to(m_new, (1, LANES))
    if not emit:
        return None
    w_prev = jnp.exp(g_prev - m_t)
    w = jnp.exp(d_mat - m_t) * _dot_nt(q, k)
    num = w_prev * _dot(q, c_old) + _dot(w, v)
    den = w_prev * jnp.sum(q.astype(F32) * n_old, axis=-1, keepdims=True) + jnp.sum(w, axis=-1, keepdims=True)
    return num / jnp.maximum(jnp.abs(den), jnp.exp(-m_t))


def _mlstm_kernel(emit_ctx,
                  q_ref, k_ref, v_ref, o_ref, g_ref, qc_ref, kc_ref, vc_ref, oc_ref, gc_ref,
                  cwq_ref, cwk_ref, cbq_ref, cbk_ref, gb_ref, ng_ref,
                  y_ref, yc_ref,
                  q_s, k_s, qc_s, kc_s, acc_s, accc_s, c_s, n_s, m_s):
    t_lat, t_ctx = q_ref.shape[1], qc_ref.shape[1]
    kscale = M_HEAD_DIM ** -0.5

    for h in range(M_HEADS):
        sl = slice(h * LANES, (h + 1) * LANES)
        for src, dst, cw, cb, sc in ((q_ref, q_s, cwq_ref, cbq_ref, 1.0), (k_ref, k_s, cwk_ref, cbk_ref, kscale),
                                     (qc_ref, qc_s, cwq_ref, cbq_ref, 1.0), (kc_ref, kc_s, cwk_ref, cbk_ref, kscale)):
            y = _silu(_dwconv3(src[0, :, sl].astype(F32), cw[:, sl], cb[:, sl]))
            dst[:, sl] = (y * sc).astype(BF16) if sc != 1.0 else y.astype(BF16)

    acc_s[...] = jnp.zeros_like(acc_s)
    accc_s[...] = jnp.zeros_like(accc_s)
    c_s[...] = jnp.zeros_like(c_s)
    n_s[...] = jnp.zeros_like(n_s)
    m_s[...] = jnp.zeros_like(m_s)

    def run(nchunks, emit, qs, ks, vr, gr, acc):
        def body(j, carry):
            for d in range(2):
                c = j if d == 0 else nchunks - 1 - j
                rows = pl.ds(pl.multiple_of(c * CHUNK, CHUNK), CHUNK)
                gt = gr[0, rows, :] + gb_ref[...]
                bt = _cumsum_rows(-_softplus(-gt), reverse=(d == 1))
                gtt, btt = gt.T, bt.T
                for h in range(M_HEADS):
                    sl = slice(h * LANES, (h + 1) * LANES)
                    out = _mlstm_stream(d, h, emit, qs[rows, sl], ks[rows, sl], vr[0, rows, sl],
                                        gt, gtt, bt, btt, c_s, n_s, m_s)
                    if emit:
                        acc[rows, sl] += out
            return carry
        lax.fori_loop(0, nchunks, body, 0)

    run(t_ctx // CHUNK, emit_ctx, qc_s, kc_s, vc_ref, gc_ref, accc_s)
    run(t_lat // CHUNK, True, q_s, k_s, v_ref, g_ref, acc_s)

    def finish(acc, oref, yref):
        for h in range(M_HEADS):
            sl = slice(h * LANES, (h + 1) * LANES)
            hn = _rms(acc[:, sl], ng_ref[:, sl])
            yref[0, :, sl] = (_sigmoid(oref[0, :, sl].astype(F32)) * hn).astype(yref.dtype)

    finish(acc_s, o_ref, y_ref)
    if emit_ctx:
        finish(accc_s, oc_ref, yc_ref)
    else:
        yc_ref[...] = jnp.zeros_like(yc_ref)


def _mlstm(u, ug, uc, ugc, conv_w, conv_b, gate_bias, norm_g, emit_ctx):
    b, t, _ = u.shape
    tc = uc.shape[1]
    w = M_WIDTH

    def ublk(tt, col):
        return pl.BlockSpec((1, tt, w), lambda bi, col=col: (bi, 0, col))

    def full(shape):
        return pl.BlockSpec(shape, lambda bi: (0,) * len(shape))

    return pl.pallas_call(
        functools.partial(_mlstm_kernel, emit_ctx),
        grid=(b,),
        in_specs=[ublk(t, 0), ublk(t, 1), ublk(t, 2), ublk(t, 3),
                  pl.BlockSpec((1, t, LANES), lambda bi: (bi, 0, 0)),
                  ublk(tc, 0), ublk(tc, 1), ublk(tc, 2), ublk(tc, 3),
                  pl.BlockSpec((1, tc, LANES), lambda bi: (bi, 0, 0)),
                  pl.BlockSpec((3, w), lambda bi: (0, 0)), pl.BlockSpec((3, w), lambda bi: (0, 1)),
                  pl.BlockSpec((1, w), lambda bi: (0, 0)), pl.BlockSpec((1, w), lambda bi: (0, 1)),
                  full((1, LANES)), full((1, w))],
        out_specs=[pl.BlockSpec((1, t, w), lambda bi: (bi, 0, 0)),
                   pl.BlockSpec((1, tc, w), lambda bi: (bi, 0, 0))],
        out_shape=[jax.ShapeDtypeStruct((b, t, w), BF16), jax.ShapeDtypeStruct((b, tc, w), BF16)],
        scratch_shapes=[pltpu.VMEM((t, w), BF16), pltpu.VMEM((t, w), BF16),
                        pltpu.VMEM((tc, w), BF16), pltpu.VMEM((tc, w), BF16),
                        pltpu.VMEM((t, w), F32), pltpu.VMEM((tc, w), F32),
                        pltpu.VMEM((2, M_HEADS, M_HEAD_DIM, M_HEAD_DIM), F32),
                        pltpu.VMEM((2, M_HEADS, 1, LANES), F32),
                        pltpu.VMEM((2, M_HEADS, 1, LANES), F32)],
        compiler_params=_cparams(("parallel",)),
        name="mlstm",
    )(u, u, u, u, ug, uc, uc, uc, uc, ugc, conv_w, conv_w, conv_b, conv_b, gate_bias, norm_g)


def _head_lanes(cols):
    rows = cols[0].shape[0]
    lane = lax.broadcasted_iota(jnp.int32, (rows, 4 * S_HEAD_DIM), 1) // S_HEAD_DIM
    out = jnp.broadcast_to(cols[3], lane.shape)
    for hh in (2, 1, 0):
        out = jnp.where(lane == hh, cols[hh], out)
    return out


def _ssd_stream(d, g, emit, xg, bg, cg, dtt, st, stt, st_s):
    tl = CHUNK - 1 if d == 0 else 0
    c0 = 16 + d * 8 + g * 4
    gw = 4 * S_HEAD_DIM
    scols = [st[:, c0 + hh:c0 + hh + 1] for hh in range(4)]
    s_l = _head_lanes(scols)
    dt_l = _head_lanes([dtt[:, c0 + hh:c0 + hh + 1] for hh in range(4)])
    s_last = s_l[tl:tl + 1, :]
    xdt = xg * dt_l
    bt4 = jnp.concatenate([bg] * 4, axis=0).T
    bt = bt4[:, 0:CHUNK]
    st_old = st_s[d, g]
    st_s[d, g] = jnp.exp(s_last) * st_old + _dot(bt, xdt * jnp.exp(s_last - s_l))
    if not emit:
        return None
    srow = jnp.concatenate([stt[c0 + hh:c0 + hh + 1, :] for hh in range(4)], axis=1)
    r = lax.broadcasted_iota(jnp.int32, (CHUNK, gw), 0)
    lane = lax.broadcasted_iota(jnp.int32, (CHUNK, gw), 1)
    sp = lane % S_HEAD_DIM
    mask = (sp <= r) if d == 0 else (sp >= r)
    seg = jnp.where(mask, jnp.exp(s_l - srow), 0.0)
    m_all = _dot(cg, bt4) * seg
    rr = lax.broadcasted_iota(jnp.int32, (gw, gw), 0) // S_HEAD_DIM
    ll = lax.broadcasted_iota(jnp.int32, (gw, gw), 1) // S_HEAD_DIM
    xd = jnp.where(rr == ll, jnp.concatenate([xdt] * 4, axis=0), 0.0)
    return _dot(m_all, xd) + jnp.exp(s_l) * _dot(cg, st_old)


def _ssd_kernel(emit_ctx,
                x_ref, bc_ref, z_ref, g_ref, xc_ref, bcc_ref, zc_ref, gc_ref,
                cwx_ref, cwb_ref, cbx_ref, cbb_ref, gb_ref, al_ref, dsk_ref, ng_ref,
                y_ref, yc_ref,
                x_s, bc_s, xc_s, bcc_s, acc_s, accc_s, st_s):
    t_lat, t_ctx = x_ref.shape[1], xc_ref.shape[1]
    gw = 4 * S_HEAD_DIM

    for c in range(S_WIDTH // LANES):
        sl = slice(c * LANES, (c + 1) * LANES)
        for src, dst, cw, cb in ((x_ref, x_s, cwx_ref, cbx_ref), (bc_ref, bc_s, cwb_ref, cbb_ref),
                                 (xc_ref, xc_s, cwx_ref, cbx_ref), (bcc_ref, bcc_s, cwb_ref, cbb_ref)):
            dst[:, sl] = _silu(_dwconv3(src[0, :, sl].astype(F32), cw[:, sl], cb[:, sl])).astype(dst.dtype)

    acc_s[...] = jnp.zeros_like(acc_s)
    accc_s[...] = jnp.zeros_like(accc_s)
    st_s[...] = jnp.zeros_like(st_s)
    acoef = -jnp.exp(al_ref[...])

    def run(nchunks, emit, xs, bcs, gr, acc):
        def body(j, carry):
            for d in range(2):
                c = j if d == 0 else nchunks - 1 - j
                rows = pl.ds(pl.multiple_of(c * CHUNK, CHUNK), CHUNK)
                dtt = _softplus(gr[0, rows, :] + gb_ref[...])
                st = _cumsum_rows(dtt * acoef, reverse=(d == 1))
                stt = st.T
                for g in range(S_GROUPS):
                    out = _ssd_stream(d, g, emit, xs[rows, g * gw:(g + 1) * gw],
                                      bcs[rows, g * S_STATE:(g + 1) * S_STATE],
                                      bcs[rows, (S_GROUPS + g) * S_STATE:(S_GROUPS + g + 1) * S_STATE],
                                      dtt, st, stt, st_s)
                    if emit:
                        acc[rows, g * gw:(g + 1) * gw] += out
            return carry
        lax.fori_loop(0, nchunks, body, 0)

    run(t_ctx // CHUNK, emit_ctx, xc_s, bcc_s, gc_ref, accc_s)
    run(t_lat // CHUNK, True, x_s, bc_s, g_ref, acc_s)

    def finish(acc, xs, zref, yref):
        for g in range(S_GROUPS):
            sl = slice(g * gw, (g + 1) * gw)
            yv = (acc[:, sl] + dsk_ref[:, sl] * xs[:, sl]) * _silu(zref[0, :, sl].astype(F32))
            yref[0, :, sl] = _rms(yv, ng_ref[:, sl]).astype(yref.dtype)

    finish(acc_s, x_s, z_ref, y_ref)
    if emit_ctx:
        finish(accc_s, xc_s, zc_ref, yc_ref)
    else:
        yc_ref[...] = jnp.zeros_like(yc_ref)


def _ssd(u, ug, uc, ugc, conv_w, conv_b, gate_bias, alog_row, dskip_row, norm_g, emit_ctx):
    b, t, _ = u.shape
    tc = uc.shape[1]
    w = S_WIDTH

    def ublk(tt, col):
        return pl.BlockSpec((1, tt, w), lambda bi, col=col: (bi, 0, col))

    def full(shape):
        return pl.BlockSpec(shape, lambda bi: (0,) * len(shape))

    return pl.pallas_call(
        functools.partial(_ssd_kernel, emit_ctx),
        grid=(b,),
        in_specs=[ublk(t, 8), ublk(t, 9), ublk(t, 7), pl.BlockSpec((1, t, LANES), lambda bi: (bi, 0, 0)),
                  ublk(tc, 8), ublk(tc, 9), ublk(tc, 7), pl.BlockSpec((1, tc, LANES), lambda bi: (bi, 0, 0)),
                  pl.BlockSpec((3, w), lambda bi: (0, 0)), pl.BlockSpec((3, w), lambda bi: (0, 1)),
                  pl.BlockSpec((1, w), lambda bi: (0, 0)), pl.BlockSpec((1, w), lambda bi: (0, 1)),
                  full((1, LANES)), full((1, LANES)), full((1, w)), full((1, w))],
        out_specs=[pl.BlockSpec((1, t, w), lambda bi: (bi, 0, 0)),
                   pl.BlockSpec((1, tc, w), lambda bi: (bi, 0, 0))],
        out_shape=[jax.ShapeDtypeStruct((b, t, w), BF16), jax.ShapeDtypeStruct((b, tc, w), BF16)],
        scratch_shapes=[pltpu.VMEM((t, w), F32), pltpu.VMEM((t, w), F32),
                        pltpu.VMEM((tc, w), F32), pltpu.VMEM((tc, w), F32),
                        pltpu.VMEM((t, w), F32), pltpu.VMEM((tc, w), F32),
                        pltpu.VMEM((2, S_GROUPS, S_STATE, 4 * S_HEAD_DIM), F32)],
        compiler_params=_cparams(("parallel",)),
        name="ssd",
    )(u, u, u, ug, uc, uc, uc, ugc, conv_w, conv_w, conv_b, conv_b, gate_bias, alog_row, dskip_row, norm_g)


def _qk_prep(x, g, cos, sin):
    lane = lax.broadcasted_iota(jnp.int32, x.shape, 1)
    lo = lane < A_SUB_DIM
    x2 = x * x
    s_lo = jnp.sum(jnp.where(lo, x2, 0.0), axis=-1, keepdims=True)
    s_hi = jnp.sum(jnp.where(lo, 0.0, x2), axis=-1, keepdims=True)
    r_lo = lax.rsqrt(s_lo * (1.0 / A_SUB_DIM) + NORM_EPS)
    r_hi = lax.rsqrt(s_hi * (1.0 / A_SUB_DIM) + NORM_EPS)
    y = x * jnp.where(lo, r_lo, r_hi) * g
    if cos is not None:
        swapped = jnp.where(lane % (2 * ROPE_FREQS) < ROPE_FREQS,
                            pltpu.roll(y, LANES - ROPE_FREQS, 1), pltpu.roll(y, ROPE_FREQS, 1))
        y = y * cos + swapped * sin
    return y


def _attn_kernel(lam_init, rope_q, src_rope, tq, *refs):
    n_src = len(src_rope)
    q_ref = refs[0]
    kv_refs = refs[1:1 + 2 * n_src]
    cos_ref, sin_ref, qg_ref, kg_ref, lam_ref, sg_ref = refs[1 + 2 * n_src:7 + 2 * n_src]
    o_ref = refs[7 + 2 * n_src]
    k0_s, k1_s, v_s = refs[8 + 2 * n_src:]
    qi = pl.program_id(1)

    @pl.when(qi == 0)
    def _():
        off = 0
        for si in range(n_src):
            k_ref, v_ref = kv_refs[2 * si], kv_refs[2 * si + 1]
            ts = k_ref.shape[1]
            for h in range(A_HEADS):
                sl = slice(h * LANES, (h + 1) * LANES)
                cs = (cos_ref[0:ts, :], sin_ref[0:ts, :]) if src_rope[si] else (None, None)
                kh = _qk_prep(k_ref[0, :, sl].astype(F32), kg_ref[...], *cs)
                lo = lax.broadcasted_iota(jnp.int32, kh.shape, 1) < A_SUB_DIM
                k0_s[h, off:off + ts, :] = jnp.where(lo, kh, 0.0).astype(BF16)
                k1_s[h, off:off + ts, :] = jnp.where(lo, 0.0, kh).astype(BF16)
                v_s[h, off:off + ts, :] = v_ref[0, :, sl]
            off += ts

    lp = lam_ref[...]
    lam = (jnp.exp(jnp.sum(lp[0:1] * lp[1:2], axis=-1, keepdims=True))
           - jnp.exp(jnp.sum(lp[2:3] * lp[3:4], axis=-1, keepdims=True)) + lam_init)
    if rope_q:
        rows = pl.ds(pl.multiple_of(qi * tq, tq), tq)
        cs = (cos_ref[rows, :], sin_ref[rows, :])
    else:
        cs = (None, None)
    for h in range(A_HEADS):
        sl = slice(h * LANES, (h + 1) * LANES)
        qh = (_qk_prep(q_ref[0, :, sl].astype(F32), qg_ref[...], *cs) * (A_SUB_DIM ** -0.5)).astype(BF16)
        s0 = _dot_nt(qh, k0_s[h])
        s1 = _dot_nt(qh, k1_s[h])
        e0 = jnp.exp(s0 - jnp.max(s0, axis=-1, keepdims=True))
        e1 = jnp.exp(s1 - jnp.max(s1, axis=-1, keepdims=True))
        r0 = 1.0 / jnp.sum(e0, axis=-1, keepdims=True)
        r1 = lam / jnp.sum(e1, axis=-1, keepdims=True)
        o = _dot(e0 * r0 - e1 * r1, v_s[h])
        o_ref[0, :, sl] = (_rms(o, sg_ref[...]) * (1.0 - lam_init)).astype(o_ref.dtype)


def _attention(uq, sources, cos, sin, qg, kg, lam_p, sub_g, lam_init, rope_q):
    b, t, _ = uq.shape
    tq = min(t, 256)
    w = A_WIDTH
    tk = sum(s[0].shape[1] for s in sources)
    in_specs = [pl.BlockSpec((1, tq, w), lambda bi, qi: (bi, qi, 4))]
    args = [uq]
    for arr, _ in sources:
        ts = arr.shape[1]
        in_specs += [pl.BlockSpec((1, ts, w), lambda bi, qi: (bi, 0, 5)),
                     pl.BlockSpec((1, ts, w), lambda bi, qi: (bi, 0, 6))]
        args += [arr, arr]
    tcs = cos.shape[0]
    for shape in ((tcs, LANES), (tcs, LANES), (1, LANES), (1, LANES), (4, A_SUB_DIM), (1, LANES)):
        in_specs.append(pl.BlockSpec(shape, lambda bi, qi: (0, 0)))
    args += [cos, sin, qg, kg, lam_p, sub_g]
    return pl.pallas_call(
        functools.partial(_attn_kernel, lam_init, rope_q, tuple(r for _, r in sources), tq),
        grid=(b, t // tq),
        in_specs=in_specs,
        out_specs=pl.BlockSpec((1, tq, w), lambda bi, qi: (bi, qi, 0)),
        out_shape=jax.ShapeDtypeStruct((b, t, w), BF16),
        scratch_shapes=[pltpu.VMEM((A_HEADS, tk, LANES), BF16)] * 3,
        compiler_params=_cparams(("parallel", "arbitrary")),
        name="diff_attention",
    )(*args)


def _merge_kernel(x_ref, ym_ref, ya_ref, ys_ref, gm_ref, ga_ref, gs_ref, m2_ref,
                  wm_ref, wa_ref, ws_ref, wo_ref, o_ref):
    t = (_sigmoid(gm_ref[0].astype(F32)) * _dot(ym_ref[0], wm_ref[...])
         + _sigmoid(ga_ref[0].astype(F32)) * _dot(ya_ref[0], wa_ref[...])
         + _sigmoid(gs_ref[0].astype(F32)) * _dot(ys_ref[0], ws_ref[...]))
    o_ref[0] = x_ref[0] + m2_ref[0] * _dot(t, wo_ref[...])


def _merge(x, y_m, y_a, y_s, u, mod2, w_m, w_a, w_s, w_o):
    b, t, _ = x.shape
    tm = min(t, 512)
    per_batch = mod2.shape[0] > 1
    mod_map = (lambda bi, i: (bi, 0, 0)) if per_batch else (lambda bi, i: (0, 0, 0))

    def yblk():
        return pl.BlockSpec((1, tm, 512), lambda bi, i: (bi, i, 0))

    def gblk(col):
        return pl.BlockSpec((1, tm, D_MODEL), lambda bi, i, col=col: (bi, i, col))

    def wblk(k):
        return pl.BlockSpec((k, D_MODEL), lambda bi, i: (0, 0))

    return pl.pallas_call(
        _merge_kernel,
        grid=(b, t // tm),
        in_specs=[pl.BlockSpec((1, tm, D_MODEL), lambda bi, i: (bi, i, 0)), yblk(), yblk(), yblk(),
                  gblk(5), gblk(6), gblk(7), pl.BlockSpec((1, 1, D_MODEL), mod_map),
                  wblk(512), wblk(512), wblk(512), wblk(D_MODEL)],
        out_specs=pl.BlockSpec((1, tm, D_MODEL), lambda bi, i: (bi, i, 0)),
        out_shape=jax.ShapeDtypeStruct(x.shape, F32),
        compiler_params=_cparams(("parallel", "parallel")),
        name="merge",
    )(x, y_m, y_a, y_s, u, u, u, mod2, w_m, w_a, w_s, w_o)


def _ffn_kernel(x_ref, sh_ref, sc_ref, gt_ref, g_ref, wa_ref, wg_ref, cwa_ref, cwg_ref, wd_ref, o_ref, h_s):
    j = pl.program_id(1)

    @pl.when(j == 0)
    def _():
        h = _rms(x_ref[0], g_ref[...]) * (1.0 + sc_ref[0]) + sh_ref[0]
        h_s[...] = h.astype(BF16)

    a = _dwconv3(jnp.dot(h_s[...], wa_ref[...], preferred_element_type=F32), cwa_ref[...])
    g = _dwconv3(jnp.dot(h_s[...], wg_ref[...], preferred_element_type=F32), cwg_ref[...])
    part = gt_ref[0] * _dot(_silu(g) * a, wd_ref[...])

    @pl.when(j == 0)
    def _():
        o_ref[0] = x_ref[0] + part

    @pl.when(j > 0)
    def _():
        o_ref[0] += part


def _ffn(x, shift, scale, gate, g, w_up, conv_w, w_down):
    b, t, _ = x.shape
    tf = 256
    nf = D_FF // tf
    per_batch = shift.shape[0] > 1
    mod_map = (lambda bi, j: (bi, 0, 0)) if per_batch else (lambda bi, j: (0, 0, 0))
    mod_spec = pl.BlockSpec((1, 1, D_MODEL), mod_map)
    return pl.pallas_call(
        _ffn_kernel,
        grid=(b, nf),
        in_specs=[pl.BlockSpec((1, t, D_MODEL), lambda bi, j: (bi, 0, 0)), mod_spec, mod_spec, mod_spec,
                  pl.BlockSpec((1, D_MODEL), lambda bi, j: (0, 0)),
                  pl.BlockSpec((D_MODEL, tf), lambda bi, j: (0, j)),
                  pl.BlockSpec((D_MODEL, tf), lambda bi, j: (0, nf + j)),
                  pl.BlockSpec((3, tf), lambda bi, j: (0, j)),
                  pl.BlockSpec((3, tf), lambda bi, j: (0, nf + j)),
                  pl.BlockSpec((tf, D_MODEL), lambda bi, j: (j, 0))],
        out_specs=pl.BlockSpec((1, t, D_MODEL), lambda bi, j: (bi, 0, 0)),
        out_shape=jax.ShapeDtypeStruct(x.shape, F32),
        scratch_shapes=[pltpu.VMEM((t, D_MODEL), BF16)],
        compiler_params=_cparams(("parallel", "arbitrary")),
        name="conv_ffn",
    )(x, shift, scale, gate, g, w_up, w_up, conv_w, conv_w, w_down)


def _rope_tables(t):
    pos = jnp.arange(t)
    inv_freq = ROPE_BASE ** (-jnp.arange(ROPE_FREQS, dtype=F32) / ROPE_FREQS)
    ang = jnp.stack([pos // GRID_W, pos % GRID_W], axis=-1).astype(F32)[..., None] * inv_freq
    cos, sin = jnp.cos(ang), jnp.sin(ang)
    cos64 = jnp.concatenate([cos[:, 0], cos[:, 0], cos[:, 1], cos[:, 1]], axis=-1)
    sin64 = jnp.concatenate([-sin[:, 0], sin[:, 0], -sin[:, 1], sin[:, 1]], axis=-1)
    return jnp.tile(cos64, (1, 2)), jnp.tile(sin64, (1, 2))


def _gate_row(m_vals, s_vals):
    return jnp.concatenate([m_vals.reshape(-1), s_vals.reshape(-1), jnp.zeros((LANES - 32,), F32)])[None, :]


def kernel(x, c, ctx, c_ctx, w_mod, b_mod, norm1_g, norm2_g, w_in, m_conv_w, m_conv_b, m_igate_b, m_fgate_b, m_norm_g, a_qnorm_g, a_knorm_g, a_lambda, a_subln_g, s_conv_w, s_conv_b, s_dt_bias, s_a_log, s_d, s_norm_g, w_branch_m, w_branch_a, w_branch_s, w_out, w_up, ffn_conv_w, w_down):
    b, t, _ = x.shape
    cos, sin = _rope_tables(t)

    c_all = jnp.concatenate([c, c_ctx[None, :], jnp.zeros((7, D_MODEL), F32)], axis=0)
    mod_all = _modulation(c_all, w_mod, b_mod)

    xc = ctx
    for l in range(DEPTH):
        emit = l < DEPTH - 1
        lam_init = 0.8 - 0.6 * math.exp(-0.3 * l)
        mod = [mod_all[l, :b, i * D_MODEL:(i + 1) * D_MODEL][:, None, :] for i in range(N_MOD)]
        modc = [mod_all[l, b:b + 1, i * D_MODEL:(i + 1) * D_MODEL][:, None, :] for i in range(N_MOD)]

        wl = w_in[l]
        w_main = jnp.concatenate([wl[:, :_MG0], wl[:, _MG1:_DT0], wl[:, _DT1:]], axis=1).astype(BF16)
        w_small = jnp.concatenate([wl[:, _MG0:_MG1], wl[:, _DT0:_DT1],
                                   jnp.zeros((D_MODEL, LANES - 32), F32)], axis=1).astype(BF16)
        gate_bias = _gate_row(jnp.stack([m_igate_b[l], m_fgate_b[l]], axis=1), s_dt_bias[l])
        alog_row = _gate_row(jnp.zeros((16,), F32), s_a_log[l])
        dskip_row = jnp.repeat(s_d[l], S_HEAD_DIM)[None, :]
        qg = jnp.tile(a_qnorm_g[l], 2)[None, :]
        kg = jnp.tile(a_knorm_g[l], 2)[None, :]

        n1 = norm1_g[l][None, :]
        u, ug = _inproj(x, mod[0], mod[1], n1, w_main, w_small)
        uc, ugc = _inproj(xc, modc[0], modc[1], n1, w_main, w_small)

        y_m, y_mc = _mlstm(u, ug, uc, ugc, m_conv_w[l], m_conv_b[l][None, :], gate_bias,
                           m_norm_g[l][None, :], emit)
        y_s, y_sc = _ssd(u, ug, uc, ugc, s_conv_w[l], s_conv_b[l][None, :], gate_bias, alog_row,
                         dskip_row, s_norm_g[l][None, :], emit)
        y_a = _attention(u, [(u, True), (uc, False)], cos, sin, qg, kg, a_lambda[l],
                         a_subln_g[l][None, :], lam_init, True)

        wm, wa, ws, wo = (w_branch_m[l].astype(BF16), w_branch_a[l].astype(BF16),
                          w_branch_s[l].astype(BF16), w_out[l].astype(BF16))
        wu, wd = w_up[l].astype(BF16), w_down[l].astype(BF16)
        n2 = norm2_g[l][None, :]
        x = _merge(x, y_m, y_a, y_s, u, mod[2], wm, wa, ws, wo)
        x = _ffn(x, mod[3], mod[4], mod[5], n2, wu, ffn_conv_w[l], wd)

        if emit:
            y_ac = _attention(uc, [(uc, False)], cos, sin, qg, kg, a_lambda[l],
                              a_subln_g[l][None, :], lam_init, False)
            xc = _merge(xc, y_mc, y_ac, y_sc, uc, modc[2], wm, wa, ws, wo)
            xc = _ffn(xc, modc[3], modc[4], modc[5], n2, wu, ffn_conv_w[l], wd)
    return x
```

```python
import functools
import math

import jax
import jax.numpy as jnp
from jax import lax
from jax.experimental import pallas as pl
from jax.experimental.pallas import tpu as pltpu

F32 = jnp.float32
BF16 = jnp.bfloat16

D_MODEL = 1024
DEPTH = 2
GRID_W = 64
M_HEADS = 4
M_HEAD_DIM = 128
M_WIDTH = 512
A_HEADS = 4
A_SUB_DIM = 64
A_V_DIM = 128
A_WIDTH = 512
ROPE_BASE = 10000.0
ROPE_FREQS = 16
S_HEADS = 8
S_HEAD_DIM = 64
S_WIDTH = 512
S_GROUPS = 2
S_STATE = 128
CHUNK = 64
D_FF = 2816
N_MOD = 6
NORM_EPS = 1e-6
LOG2E = 1.4426950408889634
LANES = 128
U_MAIN = 8192
GATE_W = 2 * LANES
DT_LANE0 = 64
CT_ROWS = M_HEAD_DIM + 16
_MG0, _MG1 = 2048, 2064
_DT0, _DT1 = 5136, 5152
VMEM_LIMIT = 56 * 1024 * 1024


def _cparams(sem):
    return pltpu.CompilerParams(dimension_semantics=sem, vmem_limit_bytes=VMEM_LIMIT)


def _sigmoid(x):
    return 1.0 / (1.0 + jnp.exp(-x))


def _silu(x):
    return x * _sigmoid(x)


def _softplus(x):
    return jnp.maximum(x, 0.0) + jnp.log1p(jnp.exp(-jnp.abs(x)))


def _dot(a, b):
    return jnp.dot(a.astype(BF16), b.astype(BF16), preferred_element_type=F32)


def _dot_nt(a, b):
    return lax.dot_general(a.astype(BF16), b.astype(BF16), (((1,), (1,)), ((), ())),
                           preferred_element_type=F32)


def _dwconv3(x, w, b=None):
    t = x.shape[0]
    row = lax.broadcasted_iota(jnp.int32, x.shape, 0)
    prev = jnp.where(row == 0, 0.0, pltpu.roll(x, 1, 0))
    nxt = jnp.where(row == t - 1, 0.0, pltpu.roll(x, t - 1, 0))
    y = prev * w[0:1] + x * w[1:2] + nxt * w[2:3]
    if b is not None:
        y = y + b
    return y


def _cumsum_rows(x, reverse):
    n = x.shape[0]
    row = lax.broadcasted_iota(jnp.int32, x.shape, 0)
    k = 1
    while k < n:
        if reverse:
            x = x + jnp.where(row < n - k, pltpu.roll(x, n - k, 0), 0.0)
        else:
            x = x + jnp.where(row >= k, pltpu.roll(x, k, 0), 0.0)
        k *= 2
    return x


def _rms(x, g):
    return x * lax.rsqrt(jnp.mean(x * x, axis=-1, keepdims=True) + NORM_EPS) * g


def _mod_kernel(c_ref, w_ref, b_ref, o_ref):
    o_ref[0] = _dot(_silu(c_ref[...]), w_ref[0]) + b_ref[0]


def _modulation(c_all, w_mod, b_mod):
    rows = c_all.shape[0]
    tn = 1536
    return pl.pallas_call(
        _mod_kernel,
        grid=(DEPTH, N_MOD * D_MODEL // tn),
        in_specs=[pl.BlockSpec((rows, D_MODEL), lambda l, j: (0, 0)),
                  pl.BlockSpec((1, D_MODEL, tn), lambda l, j: (l, 0, j)),
                  pl.BlockSpec((1, 1, tn), lambda l, j: (l, 0, j))],
        out_specs=pl.BlockSpec((1, rows, tn), lambda l, j: (l, 0, j)),
        out_shape=jax.ShapeDtypeStruct((DEPTH, rows, N_MOD * D_MODEL), F32),
        compiler_params=_cparams(("arbitrary", "arbitrary")),
        name="modulation",
    )(c_all, w_mod.astype(BF16), b_mod[:, None, :])


def _inproj_kernel(x_ref, sh_ref, sc_ref, g_ref, w_ref, ws_ref, u_ref, ug_ref, h_scr):
    @pl.when(pl.program_id(2) == 0)
    def _():
        h = _rms(x_ref[0], g_ref[...]) * (1.0 + sc_ref[0]) + sh_ref[0]
        hb = h.astype(BF16)
        h_scr[...] = hb
        ug_ref[0] = jnp.dot(hb, ws_ref[...], preferred_element_type=F32)

    u_ref[0] = jnp.dot(h_scr[...], w_ref[...], preferred_element_type=F32).astype(u_ref.dtype)


def _inproj(x, shift, scale, g, w_main, w_small):
    b, t, _ = x.shape
    tm = min(t, 1024)
    tn = 2048
    per_batch = shift.shape[0] > 1
    mod_map = (lambda bi, i, j: (bi, 0, 0)) if per_batch else (lambda bi, i, j: (0, 0, 0))
    return pl.pallas_call(
        _inproj_kernel,
        grid=(b, t // tm, U_MAIN // tn),
        in_specs=[pl.BlockSpec((1, tm, D_MODEL), lambda bi, i, j: (bi, i, 0)),
                  pl.BlockSpec((1, 1, D_MODEL), mod_map),
                  pl.BlockSpec((1, 1, D_MODEL), mod_map),
                  pl.BlockSpec((1, D_MODEL), lambda bi, i, j: (0, 0)),
                  pl.BlockSpec((D_MODEL, tn), lambda bi, i, j: (0, j)),
                  pl.BlockSpec((D_MODEL, GATE_W), lambda bi, i, j: (0, 0))],
        out_specs=[pl.BlockSpec((1, tm, tn), lambda bi, i, j: (bi, i, j)),
                   pl.BlockSpec((1, tm, GATE_W), lambda bi, i, j: (bi, i, 0))],
        out_shape=[jax.ShapeDtypeStruct((b, t, U_MAIN), BF16),
                   jax.ShapeDtypeStruct((b, t, GATE_W), F32)],
        scratch_shapes=[pltpu.VMEM((tm, D_MODEL), BF16)],
        compiler_params=_cparams(("parallel", "parallel", "arbitrary")),
        name="inproj",
    )(x, shift, scale, g, w_main, w_small)


def _cummax_rows(x, reverse):
    n = x.shape[0]
    row = lax.broadcasted_iota(jnp.int32, x.shape, 0)
    k = 1
    while k < n:
        if reverse:
            x = jnp.maximum(x, jnp.where(row < n - k, pltpu.roll(x, n - k, 0), -jnp.inf))
        else:
            x = jnp.maximum(x, jnp.where(row >= k, pltpu.roll(x, k, 0), -jnp.inf))
        k *= 2
    return x


def _mlstm_gates(d, gi, gf, m_old):
    tl = CHUNK - 1 if d == 0 else 0
    rev = d == 1
    b = _cumsum_rows(-_softplus(-gf), rev)
    ib = gi - b
    m_t = b + jnp.maximum(m_old, _cummax_rows(ib, rev))
    m_new = m_t[tl:tl + 1, :]
    b_last = b[tl:tl + 1, :]
    grp = lax.broadcasted_iota(jnp.int32, gi.shape, 1) // 8
    p = jnp.where(grp == 0, b - m_t,
                  jnp.where(grp == 1, jnp.exp(b + m_old - m_t),
                            jnp.where(grp == 2, jnp.exp(-m_t), jnp.exp(b_last - m_new + ib))))
    pt = jnp.concatenate([p, pltpu.roll(p, LANES - 1, 1)], axis=0).T
    decay = jnp.exp(b_last + m_old - m_new)
    used = lax.broadcasted_iota(jnp.int32, m_new.shape, 1) < 32
    return pt, ib, decay, jnp.where(used, m_new, 0.0)


def _mlstm_pair(d, p, emit, kp, qt, vt, pt, ib, decay, ct_s):
    def row(q):
        r = q * 8 + d * 4 + 2 * p
        return pt[r:r + 1, :]

    la, lb = d * 4 + 2 * p, d * 4 + 2 * p + 1
    lane2 = lax.broadcasted_iota(jnp.int32, (1, 2 * LANES), 1)
    decay_l = jnp.where(lane2 < LANES, decay[:, la:la + 1], decay[:, lb:lb + 1])
    klane = lax.broadcasted_iota(jnp.int32, kp.shape, 1)
    zero = jnp.zeros_like(kp)
    kbd = jnp.concatenate([jnp.where(klane < LANES, kp, zero), jnp.where(klane < LANES, zero, kp)], axis=0)
    ct_old = ct_s[d, p]
    ct_s[d, p] = decay_l * ct_old + jnp.dot((vt.astype(F32) * row(3)).astype(BF16), kbd,
                                            preferred_element_type=F32)
    if not emit:
        return None
    res = jnp.dot(jnp.concatenate([kp, ct_old.astype(BF16)], axis=0), qt, preferred_element_type=F32)
    st, cq = res[0:CHUNK], res[CHUNK:]
    r = lax.broadcasted_iota(jnp.int32, (CHUNK, LANES), 0)
    lane = lax.broadcasted_iota(jnp.int32, (CHUNK, LANES), 1)
    tt = lane % CHUNK
    mask = (r <= tt) if d == 0 else (r >= tt)
    ibp = jnp.where(lane < CHUNK, ib[:, la:la + 1], ib[:, lb:lb + 1])
    wt = jnp.where(mask, jnp.exp(row(0) + ibp), 0.0) * st
    wbd = jnp.concatenate([jnp.where(lane < CHUNK, wt, 0.0), jnp.where(lane < CHUNK, 0.0, wt)], axis=0)
    num = row(1) * cq + jnp.dot(vt, wbd.astype(BF16), preferred_element_type=F32)
    den = num[M_HEAD_DIM:M_HEAD_DIM + 1, :]
    return num[0:M_HEAD_DIM] / jnp.maximum(jnp.abs(den), row(2))


def _mlstm_kernel(emit_ctx,
                  q_ref, k_ref, v_ref, o_ref, g_ref, qc_ref, kc_ref, vc_ref, oc_ref, gc_ref,
                  cwq_ref, cwk_ref, cbq_ref, cbk_ref, gb_ref, ng_ref,
                  y_ref, yc_ref,
                  q_s, k_s, qc_s, kc_s, qt_s, vt_s, acct_s, ct_s, m_s):
    t_lat, t_ctx = q_ref.shape[1], qc_ref.shape[1]
    nc_ctx, nc_lat = t_ctx // CHUNK, t_lat // CHUNK
    kscale = M_HEAD_DIM ** -0.5

    for h in range(M_HEADS):
        sl = slice(h * LANES, (h + 1) * LANES)
        for src, dst, cw, cb, sc in ((q_ref, q_s, cwq_ref, cbq_ref, 1.0), (k_ref, k_s, cwk_ref, cbk_ref, kscale),
                                     (qc_ref, qc_s, cwq_ref, cbq_ref, 1.0), (kc_ref, kc_s, cwk_ref, cbk_ref, kscale)):
            y = _silu(_dwconv3(src[0, :, sl].astype(F32), cw[:, sl], cb[:, sl]))
            dst[:, sl] = (y * sc).astype(BF16) if sc != 1.0 else y.astype(BF16)

    def transposes(nchunks, base, qs, vr):
        def body(c, carry):
            rows = pl.ds(pl.multiple_of(c * CHUNK, CHUNK), CHUNK)
            for p in range(M_HEADS // 2):
                sl2 = slice(p * 2 * LANES, (p + 1) * 2 * LANES)
                qp = qs[rows, sl2].astype(F32)
                lane = lax.broadcasted_iota(jnp.int32, qp.shape, 1)
                qbd = jnp.concatenate([jnp.where(lane < LANES, qp, 0.0), jnp.where(lane < LANES, 0.0, qp)], axis=0)
                qt_s[base + c, p] = qbd.T.astype(BF16)
                vp = vr[0, rows, sl2].astype(F32)
                vt_s[base + c, p, 0:M_HEAD_DIM, :] = jnp.concatenate([vp[:, 0:LANES], vp[:, LANES:]],
                                                                     axis=0).T.astype(BF16)
                vt_s[base + c, p, M_HEAD_DIM:, :] = jnp.ones((CT_ROWS - M_HEAD_DIM, LANES), BF16)
                acct_s[base + c, p] = jnp.zeros((M_HEAD_DIM, LANES), F32)
            return carry
        lax.fori_loop(0, nchunks, body, 0)

    transposes(nc_ctx, 0, qc_s, vc_ref)
    transposes(nc_lat, nc_ctx, q_s, v_ref)
    ct_s[...] = jnp.zeros_like(ct_s)
    m_s[...] = jnp.zeros_like(m_s)

    def run(nchunks, base, emit, ks, gr):
        def body(j, carry):
            for d in range(2):
                c = j if d == 0 else nchunks - 1 - j
                rows = pl.ds(pl.multiple_of(c * CHUNK, CHUNK), CHUNK)
                pt, ib, decay, m_new = _mlstm_gates(d, gr[0, rows, 0:LANES] + gb_ref[:, 0:LANES],
                                                    gr[0, rows, LANES:] + gb_ref[:, LANES:], m_s[d])
                m_s[d] = m_new
                for p in range(M_HEADS // 2):
                    out = _mlstm_pair(d, p, emit, ks[rows, p * 2 * LANES:(p + 1) * 2 * LANES],
                                      qt_s[base + c, p], vt_s[base + c, p], pt, ib, decay, ct_s)
                    if emit:
                        acct_s[base + c, p] += out
            return carry
        lax.fori_loop(0, nchunks, body, 0)

    run(nc_ctx, 0, emit_ctx, kc_s, gc_ref)
    run(nc_lat, nc_ctx, True, k_s, g_ref)

    def finish(nchunks, base, oref, yref):
        def body(c, carry):
            rows = pl.ds(pl.multiple_of(c * CHUNK, CHUNK), CHUNK)
            for p in range(M_HEADS // 2):
                ht = acct_s[base + c, p].T
                for hh in range(2):
                    sl = slice((2 * p + hh) * LANES, (2 * p + hh + 1) * LANES)
                    hn = _rms(ht[hh * CHUNK:(hh + 1) * CHUNK], ng_ref[:, sl])
                    yref[0, rows, sl] = (_sigmoid(oref[0, rows, sl].astype(F32)) * hn).astype(yref.dtype)
            return carry
        lax.fori_loop(0, nchunks, body, 0)

    finish(nc_lat, nc_ctx, o_ref, y_ref)
    if emit_ctx:
        finish(nc_ctx, 0, oc_ref, yc_ref)
    else:
        yc_ref[...] = jnp.zeros_like(yc_ref)


def _mlstm(u, ug, uc, ugc, conv_w, conv_b, gate_bias, norm_g, emit_ctx):
    b, t, _ = u.shape
    tc = uc.shape[1]
    w = M_WIDTH
    nchunks = (t + tc) // CHUNK
    pairs = M_HEADS // 2

    def ublk(tt, col):
        return pl.BlockSpec((1, tt, w), lambda bi, col=col: (bi, 0, col))

    def full(shape):
        return pl.BlockSpec(shape, lambda bi: (0,) * len(shape))

    return pl.pallas_call(
        functools.partial(_mlstm_kernel, emit_ctx),
        grid=(b,),
        in_specs=[ublk(t, 0), ublk(t, 1), ublk(t, 2), ublk(t, 3),
                  pl.BlockSpec((1, t, 2 * LANES), lambda bi: (bi, 0, 0)),
                  ublk(tc, 0), ublk(tc, 1), ublk(tc, 2), ublk(tc, 3),
                  pl.BlockSpec((1, tc, 2 * LANES), lambda bi: (bi, 0, 0)),
                  pl.BlockSpec((3, w), lambda bi: (0, 0)), pl.BlockSpec((3, w), lambda bi: (0, 1)),
                  pl.BlockSpec((1, w), lambda bi: (0, 0)), pl.BlockSpec((1, w), lambda bi: (0, 1)),
                  full((1, 2 * LANES)), full((1, w))],
        out_specs=[pl.BlockSpec((1, t, w), lambda bi: (bi, 0, 0)),
                   pl.BlockSpec((1, tc, w), lambda bi: (bi, 0, 0))],
        out_shape=[jax.ShapeDtypeStruct((b, t, w), BF16), jax.ShapeDtypeStruct((b, tc, w), BF16)],
        scratch_shapes=[pltpu.VMEM((t, w), BF16), pltpu.VMEM((t, w), BF16),
                        pltpu.VMEM((tc, w), BF16), pltpu.VMEM((tc, w), BF16),
                        pltpu.VMEM((nchunks, pairs, 2 * M_HEAD_DIM, LANES), BF16),
                        pltpu.VMEM((nchunks, pairs, CT_ROWS, LANES), BF16),
                        pltpu.VMEM((nchunks, pairs, M_HEAD_DIM, LANES), F32),
                        pltpu.VMEM((2, pairs, CT_ROWS, 2 * LANES), F32),
                        pltpu.VMEM((2, 1, LANES), F32)],
        compiler_params=_cparams(("parallel",)),
        name="mlstm",
    )(u, u, u, u, ug, uc, uc, uc, uc, ugc, conv_w, conv_w, conv_b, conv_b, gate_bias, norm_g)


def _head_lanes(cols):
    rows = cols[0].shape[0]
    lane = lax.broadcasted_iota(jnp.int32, (rows, 4 * S_HEAD_DIM), 1) // S_HEAD_DIM
    out = jnp.broadcast_to(cols[3], lane.shape)
    for hh in (2, 1, 0):
        out = jnp.where(lane == hh, cols[hh], out)
    return out


def _ssd_stream(d, g, emit, xg, bg, cg, dtt, st, stt, st_s):
    tl = CHUNK - 1 if d == 0 else 0
    c0 = DT_LANE0 + d * S_HEADS + g * 4
    gw = 4 * S_HEAD_DIM
    scols = [st[:, c0 + hh:c0 + hh + 1] for hh in range(4)]
    s_l = _head_lanes(scols)
    dt_l = _head_lanes([dtt[:, c0 + hh:c0 + hh + 1] for hh in range(4)])
    s_last = s_l[tl:tl + 1, :]
    xdt = xg * dt_l
    bt4 = jnp.concatenate([bg] * 4, axis=0).T
    bt = bt4[:, 0:CHUNK]
    st_old = st_s[d, g]
    st_s[d, g] = jnp.exp(s_last) * st_old + _dot(bt, xdt * jnp.exp(s_last - s_l))
    if not emit:
        return None
    srow = jnp.concatenate([stt[c0 + hh:c0 + hh + 1, :] for hh in range(4)], axis=1)
    r = lax.broadcasted_iota(jnp.int32, (CHUNK, gw), 0)
    lane = lax.broadcasted_iota(jnp.int32, (CHUNK, gw), 1)
    sp = lane % S_HEAD_DIM
    mask = (sp <= r) if d == 0 else (sp >= r)
    seg = jnp.where(mask, jnp.exp(s_l - srow), 0.0)
    m_all = _dot(cg, bt4) * seg
    rr = lax.broadcasted_iota(jnp.int32, (gw, gw), 0) // S_HEAD_DIM
    ll = lax.broadcasted_iota(jnp.int32, (gw, gw), 1) // S_HEAD_DIM
    xd = jnp.where(rr == ll, jnp.concatenate([xdt] * 4, axis=0), 0.0)
    return _dot(m_all, xd) + jnp.exp(s_l) * _dot(cg, st_old)


def _ssd_kernel(emit_ctx,
                x_ref, bc_ref, z_ref, g_ref, xc_ref, bcc_ref, zc_ref, gc_ref,
                cwx_ref, cwb_ref, cbx_ref, cbb_ref, gb_ref, al_ref, dsk_ref, ng_ref,
                y_ref, yc_ref,
                x_s, bc_s, xc_s, bcc_s, acc_s, accc_s, st_s):
    t_lat, t_ctx = x_ref.shape[1], xc_ref.shape[1]
    gw = 4 * S_HEAD_DIM

    for c in range(S_WIDTH // LANES):
        sl = slice(c * LANES, (c + 1) * LANES)
        for src, dst, cw, cb in ((x_ref, x_s, cwx_ref, cbx_ref), (bc_ref, bc_s, cwb_ref, cbb_ref),
                                 (xc_ref, xc_s, cwx_ref, cbx_ref), (bcc_ref, bcc_s, cwb_ref, cbb_ref)):
            dst[:, sl] = _silu(_dwconv3(src[0, :, sl].astype(F32), cw[:, sl], cb[:, sl])).astype(dst.dtype)

    acc_s[...] = jnp.zeros_like(acc_s)
    accc_s[...] = jnp.zeros_like(accc_s)
    st_s[...] = jnp.zeros_like(st_s)
    acoef = -jnp.exp(al_ref[...])

    def run(nchunks, emit, xs, bcs, gr, acc):
        def body(j, carry):
            for d in range(2):
                c = j if d == 0 else nchunks - 1 - j
                rows = pl.ds(pl.multiple_of(c * CHUNK, CHUNK), CHUNK)
                dtt = _softplus(gr[0, rows, :] + gb_ref[...])
                st = _cumsum_rows(dtt * acoef, reverse=(d == 1))
                stt = st.T
                for g in range(S_GROUPS):
                    out = _ssd_stream(d, g, emit, xs[rows, g * gw:(g + 1) * gw],
                                      bcs[rows, g * S_STATE:(g + 1) * S_STATE],
                                      bcs[rows, (S_GROUPS + g) * S_STATE:(S_GROUPS + g + 1) * S_STATE],
                                      dtt, st, stt, st_s)
                    if emit:
                        acc[rows, g * gw:(g + 1) * gw] += out
            return carry
        lax.fori_loop(0, nchunks, body, 0)

    run(t_ctx // CHUNK, emit_ctx, xc_s, bcc_s, gc_ref, accc_s)
    run(t_lat // CHUNK, True, x_s, bc_s, g_ref, acc_s)

    def finish(acc, xs, zref, yref):
        for g in range(S_GROUPS):
            sl = slice(g * gw, (g + 1) * gw)
            yv = (acc[:, sl] + dsk_ref[:, sl] * xs[:, sl]) * _silu(zref[0, :, sl].astype(F32))
            yref[0, :, sl] = _rms(yv, ng_ref[:, sl]).astype(yref.dtype)

    finish(acc_s, x_s, z_ref, y_ref)
    if emit_ctx:
        finish(accc_s, xc_s, zc_ref, yc_ref)
    else:
        yc_ref[...] = jnp.zeros_like(yc_ref)


def _ssd(u, ug, uc, ugc, conv_w, conv_b, gate_bias, alog_row, dskip_row, norm_g, emit_ctx):
    b, t, _ = u.shape
    tc = uc.shape[1]
    w = S_WIDTH

    def ublk(tt, col):
        return pl.BlockSpec((1, tt, w), lambda bi, col=col: (bi, 0, col))

    def full(shape):
        return pl.BlockSpec(shape, lambda bi: (0,) * len(shape))

    return pl.pallas_call(
        functools.partial(_ssd_kernel, emit_ctx),
        grid=(b,),
        in_specs=[ublk(t, 8), ublk(t, 9), ublk(t, 7), pl.BlockSpec((1, t, LANES), lambda bi: (bi, 0, 0)),
                  ublk(tc, 8), ublk(tc, 9), ublk(tc, 7), pl.BlockSpec((1, tc, LANES), lambda bi: (bi, 0, 0)),
                  pl.BlockSpec((3, w), lambda bi: (0, 0)), pl.BlockSpec((3, w), lambda bi: (0, 1)),
                  pl.BlockSpec((1, w), lambda bi: (0, 0)), pl.BlockSpec((1, w), lambda bi: (0, 1)),
                  full((1, LANES)), full((1, LANES)), full((1, w)), full((1, w))],
        out_specs=[pl.BlockSpec((1, t, w), lambda bi: (bi, 0, 0)),
                   pl.BlockSpec((1, tc, w), lambda bi: (bi, 0, 0))],
        out_shape=[jax.ShapeDtypeStruct((b, t, w), BF16), jax.ShapeDtypeStruct((b, tc, w), BF16)],
        scratch_shapes=[pltpu.VMEM((t, w), F32), pltpu.VMEM((t, w), F32),
                        pltpu.VMEM((tc, w), F32), pltpu.VMEM((tc, w), F32),
                        pltpu.VMEM((t, w), F32), pltpu.VMEM((tc, w), F32),
                        pltpu.VMEM((2, S_GROUPS, S_STATE, 4 * S_HEAD_DIM), F32)],
        compiler_params=_cparams(("parallel",)),
        name="ssd",
    )(u, u, u, ug, uc, uc, uc, ugc, conv_w, conv_w, conv_b, conv_b, gate_bias, alog_row, dskip_row, norm_g)


def _qk_prep(x, g, cos, sin):
    lane = lax.broadcasted_iota(jnp.int32, x.shape, 1)
    lo = lane < A_SUB_DIM
    x2 = x * x
    s_lo = jnp.sum(jnp.where(lo, x2, 0.0), axis=-1, keepdims=True)
    s_hi = jnp.sum(jnp.where(lo, 0.0, x2), axis=-1, keepdims=True)
    r_lo = lax.rsqrt(s_lo * (1.0 / A_SUB_DIM) + NORM_EPS)
    r_hi = lax.rsqrt(s_hi * (1.0 / A_SUB_DIM) + NORM_EPS)
    y = x * jnp.where(lo, r_lo, r_hi) * g
    if cos is not None:
        swapped = jnp.where(lane % (2 * ROPE_FREQS) < ROPE_FREQS,
                            pltpu.roll(y, LANES - ROPE_FREQS, 1), pltpu.roll(y, ROPE_FREQS, 1))
        y = y * cos + swapped * sin
    return y


def _attn_kernel(lam_init, rope_q, src_rope, tq, *refs):
    n_src = len(src_rope)
    q_ref = refs[0]
    kv_refs = refs[1:1 + 2 * n_src]
    cos_ref, sin_ref, qg_ref, kg_ref, lam_ref, sg_ref = refs[1 + 2 * n_src:7 + 2 * n_src]
    o_ref = refs[7 + 2 * n_src]
    k_s, v_s = refs[8 + 2 * n_src:]
    qi = pl.program_id(1)

    @pl.when(qi == 0)
    def _():
        off = 0
        for si in range(n_src):
            k_ref, v_ref = kv_refs[2 * si], kv_refs[2 * si + 1]
            ts = k_ref.shape[1]
            for h in range(A_HEADS):
                sl = slice(h * LANES, (h + 1) * LANES)
                cs = (cos_ref[0:ts, :], sin_ref[0:ts, :]) if src_rope[si] else (None, None)
                k_s[h, off:off + ts, :] = _qk_prep(k_ref[0, :, sl].astype(F32), kg_ref[...], *cs).astype(BF16)
                v_s[h, off:off + ts, 0:LANES] = v_ref[0, :, sl]
                v_s[h, off:off + ts, LANES:2 * LANES] = jnp.ones((ts, LANES), BF16)
            off += ts

    lp = lam_ref[...]
    lam = (jnp.exp(jnp.sum(lp[0:1] * lp[1:2], axis=-1, keepdims=True))
           - jnp.exp(jnp.sum(lp[2:3] * lp[3:4], axis=-1, keepdims=True)) + lam_init)
    if rope_q:
        rows = pl.ds(pl.multiple_of(qi * tq, tq), tq)
        cs = (cos_ref[rows, :], sin_ref[rows, :])
    else:
        cs = (None, None)
    for h in range(A_HEADS):
        sl = slice(h * LANES, (h + 1) * LANES)
        qh = _qk_prep(q_ref[0, :, sl].astype(F32), qg_ref[...], *cs) * (A_SUB_DIM ** -0.5 * LOG2E)
        lo = lax.broadcasted_iota(jnp.int32, qh.shape, 1) < A_SUB_DIM
        q2 = jnp.concatenate([jnp.where(lo, qh, 0.0), jnp.where(lo, 0.0, qh)], axis=0).astype(BF16)
        s = _dot_nt(q2, k_s[h])
        e = jnp.exp2(s - jnp.max(s, axis=-1, keepdims=True)).astype(BF16)
        pv = jnp.dot(e, v_s[h], preferred_element_type=F32)
        o = (pv[0:tq, 0:LANES] / pv[0:tq, LANES:2 * LANES]
             - lam * (pv[tq:2 * tq, 0:LANES] / pv[tq:2 * tq, LANES:2 * LANES]))
        o_ref[0, :, sl] = (_rms(o, sg_ref[...]) * (1.0 - lam_init)).astype(o_ref.dtype)


def _attention(uq, sources, cos, sin, qg, kg, lam_p, sub_g, lam_init, rope_q):
    b, t, _ = uq.shape
    tq = min(t, 256)
    w = A_WIDTH
    tk = sum(s[0].shape[1] for s in sources)
    in_specs = [pl.BlockSpec((1, tq, w), lambda bi, qi: (bi, qi, 4))]
    args = [uq]
    for arr, _ in sources:
        ts = arr.shape[1]
        in_specs += [pl.BlockSpec((1, ts, w), lambda bi, qi: (bi, 0, 5)),
                     pl.BlockSpec((1, ts, w), lambda bi, qi: (bi, 0, 6))]
        args += [arr, arr]
    tcs = cos.shape[0]
    for shape in ((tcs, LANES), (tcs, LANES), (1, LANES), (1, LANES), (4, A_SUB_DIM), (1, LANES)):
        in_specs.append(pl.BlockSpec(shape, lambda bi, qi: (0, 0)))
    args += [cos, sin, qg, kg, lam_p, sub_g]
    return pl.pallas_call(
        functools.partial(_attn_kernel, lam_init, rope_q, tuple(r for _, r in sources), tq),
        grid=(b, t // tq),
        in_specs=in_specs,
        out_specs=pl.BlockSpec((1, tq, w), lambda bi, qi: (bi, qi, 0)),
        out_shape=jax.ShapeDtypeStruct((b, t, w), BF16),
        scratch_shapes=[pltpu.VMEM((A_HEADS, tk, LANES), BF16), pltpu.VMEM((A_HEADS, tk, 2 * LANES), BF16)],
        compiler_params=_cparams(("parallel", "arbitrary")),
        name="diff_attention",
    )(*args)


def _merge_kernel(x_ref, ym_ref, ya_ref, ys_ref, gm_ref, ga_ref, gs_ref, m2_ref,
                  wm_ref, wa_ref, ws_ref, wo_ref, o_ref):
    t = (_sigmoid(gm_ref[0].astype(F32)) * _dot(ym_ref[0], wm_ref[...])
         + _sigmoid(ga_ref[0].astype(F32)) * _dot(ya_ref[0], wa_ref[...])
         + _sigmoid(gs_ref[0].astype(F32)) * _dot(ys_ref[0], ws_ref[...]))
    o_ref[0] = x_ref[0] + m2_ref[0] * _dot(t, wo_ref[...])


def _merge(x, y_m, y_a, y_s, u, mod2, w_m, w_a, w_s, w_o):
    b, t, _ = x.shape
    tm = min(t, 512)
    per_batch = mod2.shape[0] > 1
    mod_map = (lambda bi, i: (bi, 0, 0)) if per_batch else (lambda bi, i: (0, 0, 0))

    def yblk():
        return pl.BlockSpec((1, tm, 512), lambda bi, i: (bi, i, 0))

    def gblk(col):
        return pl.BlockSpec((1, tm, D_MODEL), lambda bi, i, col=col: (bi, i, col))

    def wblk(k):
        return pl.BlockSpec((k, D_MODEL), lambda bi, i: (0, 0))

    return pl.pallas_call(
        _merge_kernel,
        grid=(b, t // tm),
        in_specs=[pl.BlockSpec((1, tm, D_MODEL), lambda bi, i: (bi, i, 0)), yblk(), yblk(), yblk(),
                  gblk(5), gblk(6), gblk(7), pl.BlockSpec((1, 1, D_MODEL), mod_map),
                  wblk(512), wblk(512), wblk(512), wblk(D_MODEL)],
        out_specs=pl.BlockSpec((1, tm, D_MODEL), lambda bi, i: (bi, i, 0)),
        out_shape=jax.ShapeDtypeStruct(x.shape, F32),
        compiler_params=_cparams(("parallel", "parallel")),
        name="merge",
    )(x, y_m, y_a, y_s, u, u, u, mod2, w_m, w_a, w_s, w_o)


def _ffn_kernel(x_ref, sh_ref, sc_ref, gt_ref, g_ref, wa_ref, wg_ref, cwa_ref, cwg_ref, wd_ref, o_ref, h_s):
    j = pl.program_id(1)
    last = pl.num_programs(1) - 1

    @pl.when(j == 0)
    def _():
        h = _rms(x_ref[0], g_ref[...]) * (1.0 + sc_ref[0]) + sh_ref[0]
        h_s[...] = h.astype(BF16)

    a = _dwconv3(jnp.dot(h_s[...], wa_ref[...], preferred_element_type=F32), cwa_ref[...])
    g = _dwconv3(jnp.dot(h_s[...], wg_ref[...], preferred_element_type=F32), cwg_ref[...])
    part = _dot(_silu(g) * a, wd_ref[...])

    @pl.when(j == 0)
    def _():
        o_ref[0] = part

    @pl.when(jnp.logical_and(j > 0, j < last))
    def _():
        o_ref[0] += part

    @pl.when(j == last)
    def _():
        o_ref[0] = x_ref[0] + gt_ref[0] * (o_ref[0] + part)


def _ffn(x, shift, scale, gate, g, w_up, conv_w, w_down):
    b, t, _ = x.shape
    tf = 256
    nf = D_FF // tf
    per_batch = shift.shape[0] > 1
    mod_map = (lambda bi, j: (bi, 0, 0)) if per_batch else (lambda bi, j: (0, 0, 0))
    mod_spec = pl.BlockSpec((1, 1, D_MODEL), mod_map)
    return pl.pallas_call(
        _ffn_kernel,
        grid=(b, nf),
        in_specs=[pl.BlockSpec((1, t, D_MODEL), lambda bi, j: (bi, 0, 0)), mod_spec, mod_spec, mod_spec,
                  pl.BlockSpec((1, D_MODEL), lambda bi, j: (0, 0)),
                  pl.BlockSpec((D_MODEL, tf), lambda bi, j: (0, j)),
                  pl.BlockSpec((D_MODEL, tf), lambda bi, j: (0, nf + j)),
                  pl.BlockSpec((3, tf), lambda bi, j: (0, j)),
                  pl.BlockSpec((3, tf), lambda bi, j: (0, nf + j)),
                  pl.BlockSpec((tf, D_MODEL), lambda bi, j: (j, 0))],
        out_specs=pl.BlockSpec((1, t, D_MODEL), lambda bi, j: (bi, 0, 0)),
        out_shape=jax.ShapeDtypeStruct(x.shape, F32),
        scratch_shapes=[pltpu.VMEM((t, D_MODEL), BF16)],
        compiler_params=_cparams(("parallel", "arbitrary")),
        name="conv_ffn",
    )(x, shift, scale, gate, g, w_up, w_up, conv_w, conv_w, w_down)


def _rope_tables(t):
    pos = jnp.arange(t)
    inv_freq = ROPE_BASE ** (-jnp.arange(ROPE_FREQS, dtype=F32) / ROPE_FREQS)
    ang = jnp.stack([pos // GRID_W, pos % GRID_W], axis=-1).astype(F32)[..., None] * inv_freq
    cos, sin = jnp.cos(ang), jnp.sin(ang)
    cos64 = jnp.concatenate([cos[:, 0], cos[:, 0], cos[:, 1], cos[:, 1]], axis=-1)
    sin64 = jnp.concatenate([-sin[:, 0], sin[:, 0], -sin[:, 1], sin[:, 1]], axis=-1)
    return jnp.tile(cos64, (1, 2)), jnp.tile(sin64, (1, 2))


def _gate_lanes():
    idx = [-1] * GATE_W
    for q in range(4):
        for d in range(2):
            for h in range(M_HEADS):
                idx[q * 8 + d * 4 + h] = d * M_HEADS + h
                idx[LANES + q * 8 + d * 4 + h] = 2 * M_HEADS + d * M_HEADS + h
    for d in range(2):
        for h in range(S_HEADS):
            idx[DT_LANE0 + d * S_HEADS + h] = 4 * M_HEADS + d * S_HEADS + h
    return idx


def _gate_row(i_vals, f_vals, dt_vals):
    idx = jnp.array(_gate_lanes())
    src = jnp.concatenate([i_vals.reshape(-1), f_vals.reshape(-1), dt_vals.reshape(-1)])
    return jnp.where(idx >= 0, src[jnp.maximum(idx, 0)], 0.0)[None, :]


def _gate_weights(wl):
    idx = jnp.array(_gate_lanes())
    mg = wl[:, _MG0:_MG1].reshape(D_MODEL, 2, 2, M_HEADS)
    src = jnp.concatenate([mg[:, :, 0].reshape(D_MODEL, -1), mg[:, :, 1].reshape(D_MODEL, -1),
                           wl[:, _DT0:_DT1]], axis=1)
    return jnp.where(idx >= 0, src[:, jnp.maximum(idx, 0)], 0.0)


def kernel(x, c, ctx, c_ctx, w_mod, b_mod, norm1_g, norm2_g, w_in, m_conv_w, m_conv_b, m_igate_b, m_fgate_b, m_norm_g, a_qnorm_g, a_knorm_g, a_lambda, a_subln_g, s_conv_w, s_conv_b, s_dt_bias, s_a_log, s_d, s_norm_g, w_branch_m, w_branch_a, w_branch_s, w_out, w_up, ffn_conv_w, w_down):
    b, t, _ = x.shape
    cos, sin = _rope_tables(t)

    c_all = jnp.concatenate([c, c_ctx[None, :], jnp.zeros((7, D_MODEL), F32)], axis=0)
    mod_all = _modulation(c_all, w_mod, b_mod)

    xc = ctx
    for l in range(DEPTH):
        emit = l < DEPTH - 1
        lam_init = 0.8 - 0.6 * math.exp(-0.3 * l)
        mod = [mod_all[l, :b, i * D_MODEL:(i + 1) * D_MODEL][:, None, :] for i in range(N_MOD)]
        modc = [mod_all[l, b:b + 1, i * D_MODEL:(i + 1) * D_MODEL][:, None, :] for i in range(N_MOD)]

        wl = w_in[l]
        w_main = jnp.concatenate([wl[:, :_MG0], wl[:, _MG1:_DT0], wl[:, _DT1:]], axis=1).astype(BF16)
        w_small = _gate_weights(wl).astype(BF16)
        zeros8 = jnp.zeros((2 * M_HEADS,), F32)
        gate_bias = _gate_row(m_igate_b[l], m_fgate_b[l], s_dt_bias[l])
        alog_row = _gate_row(zeros8, zeros8, s_a_log[l])[:, :LANES]
        dskip_row = jnp.repeat(s_d[l], S_HEAD_DIM)[None, :]
        qg = jnp.tile(a_qnorm_g[l], 2)[None, :]
        kg = jnp.tile(a_knorm_g[l], 2)[None, :]

        n1 = norm1_g[l][None, :]
        u, ug = _inproj(x, mod[0], mod[1], n1, w_main, w_small)
        uc, ugc = _inproj(xc, modc[0], modc[1], n1, w_main, w_small)

        y_m, y_mc = _mlstm(u, ug, uc, ugc, m_conv_w[l], m_conv_b[l][None, :], gate_bias,
                           m_norm_g[l][None, :], emit)
        y_s, y_sc = _ssd(u, ug, uc, ugc, s_conv_w[l], s_conv_b[l][None, :], gate_bias, alog_row,
                         dskip_row, s_norm_g[l][None, :], emit)
        y_a = _attention(u, [(u, True), (uc, False)], cos, sin, qg, kg, a_lambda[l],
                         a_subln_g[l][None, :], lam_init, True)

        wm, wa, ws, wo = (w_branch_m[l].astype(BF16), w_branch_a[l].astype(BF16),
                          w_branch_s[l].astype(BF16), w_out[l].astype(BF16))
        wu, wd = w_up[l].astype(BF16), w_down[l].astype(BF16)
        n2 = norm2_g[l][None, :]
        x = _merge(x, y_m, y_a, y_s, u, mod[2], wm, wa, ws, wo)
        x = _ffn(x, mod[3], mod[4], mod[5], n2, wu, ffn_conv_w[l], wd)

        if emit:
            y_ac = _attention(uc, [(uc, False)], cos, sin, qg, kg, a_lambda[l],
                              a_subln_g[l][None, :], lam_init, False)
            xc = _merge(xc, y_mc, y_ac, y_sc, uc, modc[2], wm, wa, ws, wo)
            xc = _ffn(xc, modc[3], modc[4], modc[5], n2, wu, ffn_conv_w[l], wd)
    return x
```

```python
import functools
import math

import jax
import jax.numpy as jnp
from jax import lax
from jax.experimental import pallas as pl
from jax.experimental.pallas import tpu as pltpu

F32 = jnp.float32
BF16 = jnp.bfloat16

D_MODEL = 1024
DEPTH = 2
GRID_W = 64
M_HEADS = 4
M_HEAD_DIM = 128
M_WIDTH = 512
A_HEADS = 4
A_SUB_DIM = 64
A_V_DIM = 128
A_WIDTH = 512
ROPE_BASE = 10000.0
ROPE_FREQS = 16
S_HEADS = 8
S_HEAD_DIM = 64
S_WIDTH = 512
S_GROUPS = 2
S_STATE = 128
CHUNK = 64
D_FF = 2816
N_MOD = 6
NORM_EPS = 1e-6
LOG2E = 1.4426950408889634
LANES = 128
U_MAIN = 8192
GATE_W = 2 * LANES
DT_LANE0 = 64
CT_ROWS = M_HEAD_DIM + 16
_MG0, _MG1 = 2048, 2064
_DT0, _DT1 = 5136, 5152
VMEM_LIMIT = 56 * 1024 * 1024


def _cparams(sem):
    return pltpu.CompilerParams(dimension_semantics=sem, vmem_limit_bytes=VMEM_LIMIT)


def _sigmoid(x):
    return 1.0 / (1.0 + jnp.exp(-x))


def _silu(x):
    return x * _sigmoid(x)


def _softplus(x):
    return jnp.maximum(x, 0.0) + jnp.log1p(jnp.exp(-jnp.abs(x)))


def _dot(a, b):
    return jnp.dot(a.astype(BF16), b.astype(BF16), preferred_element_type=F32)


def _dot_nt(a, b):
    return lax.dot_general(a.astype(BF16), b.astype(BF16), (((1,), (1,)), ((), ())),
                           preferred_element_type=F32)


def _dwconv3(x, w, b=None, period=None):
    t = x.shape[0]
    row = lax.broadcasted_iota(jnp.int32, x.shape, 0)
    if period is not None and period < t:
        row = row % period
        last = period - 1
    else:
        last = t - 1
    prev = jnp.where(row == 0, 0.0, pltpu.roll(x, 1, 0))
    nxt = jnp.where(row == last, 0.0, pltpu.roll(x, t - 1, 0))
    y = prev * w[0:1] + x * w[1:2] + nxt * w[2:3]
    if b is not None:
        y = y + b
    return y


def _cumsum_rows(x, reverse):
    n = x.shape[0]
    row = lax.broadcasted_iota(jnp.int32, x.shape, 0)
    k = 1
    while k < n:
        if reverse:
            x = x + jnp.where(row < n - k, pltpu.roll(x, n - k, 0), 0.0)
        else:
            x = x + jnp.where(row >= k, pltpu.roll(x, k, 0), 0.0)
        k *= 2
    return x


def _rms(x, g):
    return x * lax.rsqrt(jnp.mean(x * x, axis=-1, keepdims=True) + NORM_EPS) * g


def _mod_kernel(c_ref, w_ref, b_ref, o_ref):
    o_ref[0] = _dot(_silu(c_ref[...]), w_ref[0]) + b_ref[0]


def _modulation(c_all, w_mod, b_mod):
    rows = c_all.shape[0]
    tn = 1536
    return pl.pallas_call(
        _mod_kernel,
        grid=(DEPTH, N_MOD * D_MODEL // tn),
        in_specs=[pl.BlockSpec((rows, D_MODEL), lambda l, j: (0, 0)),
                  pl.BlockSpec((1, D_MODEL, tn), lambda l, j: (l, 0, j)),
                  pl.BlockSpec((1, 1, tn), lambda l, j: (l, 0, j))],
        out_specs=pl.BlockSpec((1, rows, tn), lambda l, j: (l, 0, j)),
        out_shape=jax.ShapeDtypeStruct((DEPTH, rows, N_MOD * D_MODEL), F32),
        compiler_params=_cparams(("arbitrary", "arbitrary")),
        name="modulation",
    )(c_all, w_mod.astype(BF16), b_mod[:, None, :])


def _inproj_kernel(x_ref, sh_ref, sc_ref, g_ref, w_ref, ws_ref, u_ref, ug_ref, h_scr):
    @pl.when(pl.program_id(2) == 0)
    def _():
        h = _rms(x_ref[0], g_ref[...]) * (1.0 + sc_ref[0]) + sh_ref[0]
        hb = h.astype(BF16)
        h_scr[...] = hb
        ug_ref[0] = jnp.dot(hb, ws_ref[...], preferred_element_type=F32)

    u_ref[0] = jnp.dot(h_scr[...], w_ref[...], preferred_element_type=F32).astype(u_ref.dtype)


def _inproj(x, shift, scale, g, w_main, w_small):
    b, t, _ = x.shape
    tm = min(t, 1024)
    tn = 2048
    per_batch = shift.shape[0] > 1
    mod_map = (lambda bi, i, j: (bi, 0, 0)) if per_batch else (lambda bi, i, j: (0, 0, 0))
    return pl.pallas_call(
        _inproj_kernel,
        grid=(b, t // tm, U_MAIN // tn),
        in_specs=[pl.BlockSpec((1, tm, D_MODEL), lambda bi, i, j: (bi, i, 0)),
                  pl.BlockSpec((1, 1, D_MODEL), mod_map),
                  pl.BlockSpec((1, 1, D_MODEL), mod_map),
                  pl.BlockSpec((1, D_MODEL), lambda bi, i, j: (0, 0)),
                  pl.BlockSpec((D_MODEL, tn), lambda bi, i, j: (0, j)),
                  pl.BlockSpec((D_MODEL, GATE_W), lambda bi, i, j: (0, 0))],
        out_specs=[pl.BlockSpec((1, tm, tn), lambda bi, i, j: (bi, i, j)),
                   pl.BlockSpec((1, tm, GATE_W), lambda bi, i, j: (bi, i, 0))],
        out_shape=[jax.ShapeDtypeStruct((b, t, U_MAIN), BF16),
                   jax.ShapeDtypeStruct((b, t, GATE_W), F32)],
        scratch_shapes=[pltpu.VMEM((tm, D_MODEL), BF16)],
        compiler_params=_cparams(("parallel", "parallel", "arbitrary")),
        name="inproj",
    )(x, shift, scale, g, w_main, w_small)


def _cummax_rows(x, reverse):
    n = x.shape[0]
    row = lax.broadcasted_iota(jnp.int32, x.shape, 0)
    k = 1
    while k < n:
        if reverse:
            x = jnp.maximum(x, jnp.where(row < n - k, pltpu.roll(x, n - k, 0), -jnp.inf))
        else:
            x = jnp.maximum(x, jnp.where(row >= k, pltpu.roll(x, k, 0), -jnp.inf))
        k *= 2
    return x


def _mlstm_gates(d, gi, gf, m_old):
    tl = CHUNK - 1 if d == 0 else 0
    rev = d == 1
    b = _cumsum_rows(-_softplus(-gf), rev)
    ib = gi - b
    m_t = b + jnp.maximum(m_old, _cummax_rows(ib, rev))
    m_new = m_t[tl:tl + 1, :]
    b_last = b[tl:tl + 1, :]
    grp = lax.broadcasted_iota(jnp.int32, gi.shape, 1) // 8
    p = jnp.where(grp == 0, b - m_t,
                  jnp.where(grp == 1, jnp.exp(b + m_old - m_t),
                            jnp.where(grp == 2, jnp.exp(-m_t), jnp.exp(b_last - m_new + ib))))
    pt = jnp.concatenate([p, pltpu.roll(p, LANES - 1, 1)], axis=0).T
    decay = jnp.exp(b_last + m_old - m_new)
    used = lax.broadcasted_iota(jnp.int32, m_new.shape, 1) < 32
    return pt, ib, decay, jnp.where(used, m_new, 0.0)


def _mlstm_pair(d, p, emit, kp, qt, vt, pt, ib, decay, ct_s):
    def row(q):
        r = q * 8 + d * 4 + 2 * p
        return pt[r:r + 1, :]

    la, lb = d * 4 + 2 * p, d * 4 + 2 * p + 1
    lane2 = lax.broadcasted_iota(jnp.int32, (1, 2 * LANES), 1)
    decay_l = jnp.where(lane2 < LANES, decay[:, la:la + 1], decay[:, lb:lb + 1])
    klane = lax.broadcasted_iota(jnp.int32, kp.shape, 1)
    zero = jnp.zeros_like(kp)
    kbd = jnp.concatenate([jnp.where(klane < LANES, kp, zero), jnp.where(klane < LANES, zero, kp)], axis=0)
    ct_old = ct_s[d, p]
    ct_s[d, p] = decay_l * ct_old + jnp.dot((vt.astype(F32) * row(3)).astype(BF16), kbd,
                                            preferred_element_type=F32)
    if not emit:
        return None
    res = jnp.dot(jnp.concatenate([kp, ct_old.astype(BF16)], axis=0), qt, preferred_element_type=F32)
    st, cq = res[0:CHUNK], res[CHUNK:]
    r = lax.broadcasted_iota(jnp.int32, (CHUNK, LANES), 0)
    lane = lax.broadcasted_iota(jnp.int32, (CHUNK, LANES), 1)
    tt = lane % CHUNK
    mask = (r <= tt) if d == 0 else (r >= tt)
    ibp = jnp.where(lane < CHUNK, ib[:, la:la + 1], ib[:, lb:lb + 1])
    wt = jnp.where(mask, jnp.exp(row(0) + ibp), 0.0) * st
    wbd = jnp.concatenate([jnp.where(lane < CHUNK, wt, 0.0), jnp.where(lane < CHUNK, 0.0, wt)], axis=0)
    num = row(1) * cq + jnp.dot(vt, wbd.astype(BF16), preferred_element_type=F32)
    den = num[M_HEAD_DIM:M_HEAD_DIM + 1, :]
    return num[0:M_HEAD_DIM] / jnp.maximum(jnp.abs(den), row(2))


def _mlstm_kernel(emit_ctx,
                  q_ref, k_ref, v_ref, o_ref, g_ref, qc_ref, kc_ref, vc_ref, oc_ref, gc_ref,
                  cwq_ref, cwk_ref, cbq_ref, cbk_ref, gb_ref, ng_ref,
                  y_ref, yc_ref,
                  q_s, k_s, qc_s, kc_s, qt_s, vt_s, acct_s, ct_s, m_s):
    t_lat, t_ctx = q_ref.shape[1], qc_ref.shape[1]
    nc_ctx, nc_lat = t_ctx // CHUNK, t_lat // CHUNK
    kscale = M_HEAD_DIM ** -0.5

    for h in range(M_HEADS):
        sl = slice(h * LANES, (h + 1) * LANES)
        for src, dst, cw, cb, sc in ((q_ref, q_s, cwq_ref, cbq_ref, 1.0), (k_ref, k_s, cwk_ref, cbk_ref, kscale),
                                     (qc_ref, qc_s, cwq_ref, cbq_ref, 1.0), (kc_ref, kc_s, cwk_ref, cbk_ref, kscale)):
            y = _silu(_dwconv3(src[0, :, sl].astype(F32), cw[:, sl], cb[:, sl]))
            dst[:, sl] = (y * sc).astype(BF16) if sc != 1.0 else y.astype(BF16)

    def transposes(nchunks, base, qs, vr):
        def body(c, carry):
            rows = pl.ds(pl.multiple_of(c * CHUNK, CHUNK), CHUNK)
            for p in range(M_HEADS // 2):
                sl2 = slice(p * 2 * LANES, (p + 1) * 2 * LANES)
                qp = qs[rows, sl2].astype(F32)
                lane = lax.broadcasted_iota(jnp.int32, qp.shape, 1)
                qbd = jnp.concatenate([jnp.where(lane < LANES, qp, 0.0), jnp.where(lane < LANES, 0.0, qp)], axis=0)
                qt_s[base + c, p] = qbd.T.astype(BF16)
                vp = vr[0, rows, sl2].astype(F32)
                vt_s[base + c, p, 0:M_HEAD_DIM, :] = jnp.concatenate([vp[:, 0:LANES], vp[:, LANES:]],
                                                                     axis=0).T.astype(BF16)
                vt_s[base + c, p, M_HEAD_DIM:, :] = jnp.ones((CT_ROWS - M_HEAD_DIM, LANES), BF16)
                acct_s[base + c, p] = jnp.zeros((M_HEAD_DIM, LANES), F32)
            return carry
        lax.fori_loop(0, nchunks, body, 0, unroll=2)

    transposes(nc_ctx, 0, qc_s, vc_ref)
    transposes(nc_lat, nc_ctx, q_s, v_ref)
    ct_s[...] = jnp.zeros_like(ct_s)
    m_s[...] = jnp.zeros_like(m_s)

    def run(nchunks, base, emit, ks, gr):
        def body(j, carry):
            for d in range(2):
                c = j if d == 0 else nchunks - 1 - j
                rows = pl.ds(pl.multiple_of(c * CHUNK, CHUNK), CHUNK)
                pt, ib, decay, m_new = _mlstm_gates(d, gr[0, rows, 0:LANES] + gb_ref[:, 0:LANES],
                                                    gr[0, rows, LANES:] + gb_ref[:, LANES:], m_s[d])
                m_s[d] = m_new
                for p in range(M_HEADS // 2):
                    out = _mlstm_pair(d, p, emit, ks[rows, p * 2 * LANES:(p + 1) * 2 * LANES],
                                      qt_s[base + c, p], vt_s[base + c, p], pt, ib, decay, ct_s)
                    if emit:
                        acct_s[base + c, p] += out
            return carry
        lax.fori_loop(0, nchunks, body, 0, unroll=2)

    run(nc_ctx, 0, emit_ctx, kc_s, gc_ref)
    run(nc_lat, nc_ctx, True, k_s, g_ref)

    def finish(nchunks, base, oref, yref):
        def body(c, carry):
            rows = pl.ds(pl.multiple_of(c * CHUNK, CHUNK), CHUNK)
            for p in range(M_HEADS // 2):
                ht = acct_s[base + c, p].T
                for hh in range(2):
                    sl = slice((2 * p + hh) * LANES, (2 * p + hh + 1) * LANES)
                    hn = _rms(ht[hh * CHUNK:(hh + 1) * CHUNK], ng_ref[:, sl])
                    yref[0, rows, sl] = (_sigmoid(oref[0, rows, sl].astype(F32)) * hn).astype(yref.dtype)
            return carry
        lax.fori_loop(0, nchunks, body, 0, unroll=2)

    finish(nc_lat, nc_ctx, o_ref, y_ref)
    if emit_ctx:
        finish(nc_ctx, 0, oc_ref, yc_ref)
    else:
        yc_ref[...] = jnp.zeros_like(yc_ref)


def _mlstm(u, ug, uc, ugc, conv_w, conv_b, gate_bias, norm_g, emit_ctx):
    b, t, _ = u.shape
    tc = uc.shape[1]
    w = M_WIDTH
    nchunks = (t + tc) // CHUNK
    pairs = M_HEADS // 2

    def ublk(tt, col):
        return pl.BlockSpec((1, tt, w), lambda bi, col=col: (bi, 0, col))

    def full(shape):
        return pl.BlockSpec(shape, lambda bi: (0,) * len(shape))

    return pl.pallas_call(
        functools.partial(_mlstm_kernel, emit_ctx),
        grid=(b,),
        in_specs=[ublk(t, 0), ublk(t, 1), ublk(t, 2), ublk(t, 3),
                  pl.BlockSpec((1, t, 2 * LANES), lambda bi: (bi, 0, 0)),
                  ublk(tc, 0), ublk(tc, 1), ublk(tc, 2), ublk(tc, 3),
                  pl.BlockSpec((1, tc, 2 * LANES), lambda bi: (bi, 0, 0)),
                  pl.BlockSpec((3, w), lambda bi: (0, 0)), pl.BlockSpec((3, w), lambda bi: (0, 1)),
                  pl.BlockSpec((1, w), lambda bi: (0, 0)), pl.BlockSpec((1, w), lambda bi: (0, 1)),
                  full((1, 2 * LANES)), full((1, w))],
        out_specs=[pl.BlockSpec((1, t, w), lambda bi: (bi, 0, 0)),
                   pl.BlockSpec((1, tc, w), lambda bi: (bi, 0, 0))],
        out_shape=[jax.ShapeDtypeStruct((b, t, w), BF16), jax.ShapeDtypeStruct((b, tc, w), BF16)],
        scratch_shapes=[pltpu.VMEM((t, w), BF16), pltpu.VMEM((t, w), BF16),
                        pltpu.VMEM((tc, w), BF16), pltpu.VMEM((tc, w), BF16),
                        pltpu.VMEM((nchunks, pairs, 2 * M_HEAD_DIM, LANES), BF16),
                        pltpu.VMEM((nchunks, pairs, CT_ROWS, LANES), BF16),
                        pltpu.VMEM((nchunks, pairs, M_HEAD_DIM, LANES), F32),
                        pltpu.VMEM((2, pairs, CT_ROWS, 2 * LANES), F32),
                        pltpu.VMEM((2, 1, LANES), F32)],
        compiler_params=_cparams(("parallel",)),
        name="mlstm",
    )(u, u, u, u, ug, uc, uc, uc, uc, ugc, conv_w, conv_w, conv_b, conv_b, gate_bias, norm_g)


def _head_lanes(cols):
    rows = cols[0].shape[0]
    lane = lax.broadcasted_iota(jnp.int32, (rows, 4 * S_HEAD_DIM), 1) // S_HEAD_DIM
    out = jnp.broadcast_to(cols[3], lane.shape)
    for hh in (2, 1, 0):
        out = jnp.where(lane == hh, cols[hh], out)
    return out


def _ssd_stream(d, g, emit, xg, bg, cg, dtt, st, stt, st_s):
    tl = CHUNK - 1 if d == 0 else 0
    c0 = DT_LANE0 + d * S_HEADS + g * 4
    gw = 4 * S_HEAD_DIM
    scols = [st[:, c0 + hh:c0 + hh + 1] for hh in range(4)]
    s_l = _head_lanes(scols)
    dt_l = _head_lanes([dtt[:, c0 + hh:c0 + hh + 1] for hh in range(4)])
    s_last = s_l[tl:tl + 1, :]
    xdt = xg * dt_l
    bt4 = jnp.concatenate([bg] * 4, axis=0).T
    bt = bt4[:, 0:CHUNK]
    st_old = st_s[d, g]
    st_s[d, g] = jnp.exp(s_last) * st_old + _dot(bt, xdt * jnp.exp(s_last - s_l))
    if not emit:
        return None
    srow = jnp.concatenate([stt[c0 + hh:c0 + hh + 1, :] for hh in range(4)], axis=1)
    r = lax.broadcasted_iota(jnp.int32, (CHUNK, gw), 0)
    lane = lax.broadcasted_iota(jnp.int32, (CHUNK, gw), 1)
    sp = lane % S_HEAD_DIM
    mask = (sp <= r) if d == 0 else (sp >= r)
    seg = jnp.where(mask, jnp.exp(s_l - srow), 0.0)
    m_all = _dot(cg, bt4) * seg
    rr = lax.broadcasted_iota(jnp.int32, (gw, gw), 0) // S_HEAD_DIM
    ll = lax.broadcasted_iota(jnp.int32, (gw, gw), 1) // S_HEAD_DIM
    xd = jnp.where(rr == ll, jnp.concatenate([xdt] * 4, axis=0), 0.0)
    return _dot(m_all, xd) + jnp.exp(s_l) * _dot(cg, st_old)


def _ssd_kernel(emit_ctx,
                x_ref, bc_ref, z_ref, g_ref, xc_ref, bcc_ref, zc_ref, gc_ref,
                cwx_ref, cwb_ref, cbx_ref, cbb_ref, gb_ref, al_ref, dsk_ref, ng_ref,
                y_ref, yc_ref,
                x_s, bc_s, xc_s, bcc_s, acc_s, accc_s, st_s):
    t_lat, t_ctx = x_ref.shape[1], xc_ref.shape[1]
    gw = 4 * S_HEAD_DIM

    for c in range(S_WIDTH // LANES):
        sl = slice(c * LANES, (c + 1) * LANES)
        for src, dst, cw, cb in ((x_ref, x_s, cwx_ref, cbx_ref), (bc_ref, bc_s, cwb_ref, cbb_ref),
                                 (xc_ref, xc_s, cwx_ref, cbx_ref), (bcc_ref, bcc_s, cwb_ref, cbb_ref)):
            dst[:, sl] = _silu(_dwconv3(src[0, :, sl].astype(F32), cw[:, sl], cb[:, sl])).astype(dst.dtype)

    acc_s[...] = jnp.zeros_like(acc_s)
    accc_s[...] = jnp.zeros_like(accc_s)
    st_s[...] = jnp.zeros_like(st_s)
    acoef = -jnp.exp(al_ref[...])

    def run(nchunks, emit, xs, bcs, gr, acc):
        def body(j, carry):
            for d in range(2):
                c = j if d == 0 else nchunks - 1 - j
                rows = pl.ds(pl.multiple_of(c * CHUNK, CHUNK), CHUNK)
                dtt = _softplus(gr[0, rows, :] + gb_ref[...])
                st = _cumsum_rows(dtt * acoef, reverse=(d == 1))
                stt = st.T
                for g in range(S_GROUPS):
                    out = _ssd_stream(d, g, emit, xs[rows, g * gw:(g + 1) * gw],
                                      bcs[rows, g * S_STATE:(g + 1) * S_STATE],
                                      bcs[rows, (S_GROUPS + g) * S_STATE:(S_GROUPS + g + 1) * S_STATE],
                                      dtt, st, stt, st_s)
                    if emit:
                        acc[rows, g * gw:(g + 1) * gw] += out
            return carry
        lax.fori_loop(0, nchunks, body, 0, unroll=2)

    run(t_ctx // CHUNK, emit_ctx, xc_s, bcc_s, gc_ref, accc_s)
    run(t_lat // CHUNK, True, x_s, bc_s, g_ref, acc_s)

    def finish(acc, xs, zref, yref):
        for g in range(S_GROUPS):
            sl = slice(g * gw, (g + 1) * gw)
            yv = (acc[:, sl] + dsk_ref[:, sl] * xs[:, sl]) * _silu(zref[0, :, sl].astype(F32))
            yref[0, :, sl] = _rms(yv, ng_ref[:, sl]).astype(yref.dtype)

    finish(acc_s, x_s, z_ref, y_ref)
    if emit_ctx:
        finish(accc_s, xc_s, zc_ref, yc_ref)
    else:
        yc_ref[...] = jnp.zeros_like(yc_ref)


def _ssd(u, ug, uc, ugc, conv_w, conv_b, gate_bias, alog_row, dskip_row, norm_g, emit_ctx):
    b, t, _ = u.shape
    tc = uc.shape[1]
    w = S_WIDTH

    def ublk(tt, col):
        return pl.BlockSpec((1, tt, w), lambda bi, col=col: (bi, 0, col))

    def full(shape):
        return pl.BlockSpec(shape, lambda bi: (0,) * len(shape))

    return pl.pallas_call(
        functools.partial(_ssd_kernel, emit_ctx),
        grid=(b,),
        in_specs=[ublk(t, 8), ublk(t, 9), ublk(t, 7), pl.BlockSpec((1, t, LANES), lambda bi: (bi, 0, 0)),
                  ublk(tc, 8), ublk(tc, 9), ublk(tc, 7), pl.BlockSpec((1, tc, LANES), lambda bi: (bi, 0, 0)),
                  pl.BlockSpec((3, w), lambda bi: (0, 0)), pl.BlockSpec((3, w), lambda bi: (0, 1)),
                  pl.BlockSpec((1, w), lambda bi: (0, 0)), pl.BlockSpec((1, w), lambda bi: (0, 1)),
                  full((1, LANES)), full((1, LANES)), full((1, w)), full((1, w))],
        out_specs=[pl.BlockSpec((1, t, w), lambda bi: (bi, 0, 0)),
                   pl.BlockSpec((1, tc, w), lambda bi: (bi, 0, 0))],
        out_shape=[jax.ShapeDtypeStruct((b, t, w), BF16), jax.ShapeDtypeStruct((b, tc, w), BF16)],
        scratch_shapes=[pltpu.VMEM((t, w), F32), pltpu.VMEM((t, w), F32),
                        pltpu.VMEM((tc, w), F32), pltpu.VMEM((tc, w), F32),
                        pltpu.VMEM((t, w), F32), pltpu.VMEM((tc, w), F32),
                        pltpu.VMEM((2, S_GROUPS, S_STATE, 4 * S_HEAD_DIM), F32)],
        compiler_params=_cparams(("parallel",)),
        name="ssd",
    )(u, u, u, ug, uc, uc, uc, ugc, conv_w, conv_w, conv_b, conv_b, gate_bias, alog_row, dskip_row, norm_g)


def _qk_prep(x, g, cos, sin):
    lane = lax.broadcasted_iota(jnp.int32, x.shape, 1)
    lo = lane < A_SUB_DIM
    x2 = x * x
    s_lo = jnp.sum(jnp.where(lo, x2, 0.0), axis=-1, keepdims=True)
    s_hi = jnp.sum(jnp.where(lo, 0.0, x2), axis=-1, keepdims=True)
    r_lo = lax.rsqrt(s_lo * (1.0 / A_SUB_DIM) + NORM_EPS)
    r_hi = lax.rsqrt(s_hi * (1.0 / A_SUB_DIM) + NORM_EPS)
    y = x * jnp.where(lo, r_lo, r_hi) * g
    if cos is not None:
        swapped = jnp.where(lane % (2 * ROPE_FREQS) < ROPE_FREQS,
                            pltpu.roll(y, LANES - ROPE_FREQS, 1), pltpu.roll(y, ROPE_FREQS, 1))
        y = y * cos + swapped * sin
    return y


def _attn_kernel(lam_init, rope_q, src_rope, tq, *refs):
    n_src = len(src_rope)
    q_ref = refs[0]
    kv_refs = refs[1:1 + 2 * n_src]
    cos_ref, sin_ref, qg_ref, kg_ref, lam_ref, sg_ref = refs[1 + 2 * n_src:7 + 2 * n_src]
    o_ref = refs[7 + 2 * n_src]
    k_s, v_s = refs[8 + 2 * n_src:]
    qi = pl.program_id(1)

    @pl.when(qi == 0)
    def _():
        off = 0
        for si in range(n_src):
            k_ref, v_ref = kv_refs[2 * si], kv_refs[2 * si + 1]
            ts = k_ref.shape[1]
            for h in range(A_HEADS):
                sl = slice(h * LANES, (h + 1) * LANES)
                cs = (cos_ref[0:ts, :], sin_ref[0:ts, :]) if src_rope[si] else (None, None)
                k_s[h, off:off + ts, :] = _qk_prep(k_ref[0, :, sl].astype(F32), kg_ref[...], *cs).astype(BF16)
                v_s[h, off:off + ts, 0:LANES] = v_ref[0, :, sl]
                v_s[h, off:off + ts, LANES:2 * LANES] = jnp.ones((ts, LANES), BF16)
            off += ts

    lp = lam_ref[...]
    lam = (jnp.exp(jnp.sum(lp[0:1] * lp[1:2], axis=-1, keepdims=True))
           - jnp.exp(jnp.sum(lp[2:3] * lp[3:4], axis=-1, keepdims=True)) + lam_init)
    if rope_q:
        rows = pl.ds(pl.multiple_of(qi * tq, tq), tq)
        cs = (cos_ref[rows, :], sin_ref[rows, :])
    else:
        cs = (None, None)
    for h in range(A_HEADS):
        sl = slice(h * LANES, (h + 1) * LANES)
        qh = _qk_prep(q_ref[0, :, sl].astype(F32), qg_ref[...], *cs) * (A_SUB_DIM ** -0.5 * LOG2E)
        lo = lax.broadcasted_iota(jnp.int32, qh.shape, 1) < A_SUB_DIM
        q2 = jnp.concatenate([jnp.where(lo, qh, 0.0), jnp.where(lo, 0.0, qh)], axis=0).astype(BF16)
        s = _dot_nt(q2, k_s[h])
        e = jnp.exp2(s - jnp.max(s, axis=-1, keepdims=True)).astype(BF16)
        pv = jnp.dot(e, v_s[h], preferred_element_type=F32)
        o = (pv[0:tq, 0:LANES] / pv[0:tq, LANES:2 * LANES]
             - lam * (pv[tq:2 * tq, 0:LANES] / pv[tq:2 * tq, LANES:2 * LANES]))
        o_ref[0, :, sl] = (_rms(o, sg_ref[...]) * (1.0 - lam_init)).astype(o_ref.dtype)


def _attention(uq, sources, cos, sin, qg, kg, lam_p, sub_g, lam_init, rope_q):
    b, t, _ = uq.shape
    tq = min(t, 256)
    w = A_WIDTH
    tk = sum(s[0].shape[1] for s in sources)
    in_specs = [pl.BlockSpec((1, tq, w), lambda bi, qi: (bi, qi, 4))]
    args = [uq]
    for arr, _ in sources:
        ts = arr.shape[1]
        in_specs += [pl.BlockSpec((1, ts, w), lambda bi, qi: (bi, 0, 5)),
                     pl.BlockSpec((1, ts, w), lambda bi, qi: (bi, 0, 6))]
        args += [arr, arr]
    tcs = cos.shape[0]
    for shape in ((tcs, LANES), (tcs, LANES), (1, LANES), (1, LANES), (4, A_SUB_DIM), (1, LANES)):
        in_specs.append(pl.BlockSpec(shape, lambda bi, qi: (0, 0)))
    args += [cos, sin, qg, kg, lam_p, sub_g]
    return pl.pallas_call(
        functools.partial(_attn_kernel, lam_init, rope_q, tuple(r for _, r in sources), tq),
        grid=(b, t // tq),
        in_specs=in_specs,
        out_specs=pl.BlockSpec((1, tq, w), lambda bi, qi: (bi, qi, 0)),
        out_shape=jax.ShapeDtypeStruct((b, t, w), BF16),
        scratch_shapes=[pltpu.VMEM((A_HEADS, tk, LANES), BF16), pltpu.VMEM((A_HEADS, tk, 2 * LANES), BF16)],
        compiler_params=_cparams(("parallel", "arbitrary")),
        name="diff_attention",
    )(*args)


def _merge_kernel(x_ref, ym_ref, ya_ref, ys_ref, gm_ref, ga_ref, gs_ref, m2_ref,
                  wm_ref, wa_ref, ws_ref, wo_ref, o_ref):
    t = (_sigmoid(gm_ref[0].astype(F32)) * _dot(ym_ref[0], wm_ref[...])
         + _sigmoid(ga_ref[0].astype(F32)) * _dot(ya_ref[0], wa_ref[...])
         + _sigmoid(gs_ref[0].astype(F32)) * _dot(ys_ref[0], ws_ref[...]))
    o_ref[0] = x_ref[0] + m2_ref[0] * _dot(t, wo_ref[...])


def _merge(x, y_m, y_a, y_s, u, mod2, w_m, w_a, w_s, w_o):
    b, t, _ = x.shape
    tm = min(t, 512)
    per_batch = mod2.shape[0] > 1
    mod_map = (lambda bi, i: (bi, 0, 0)) if per_batch else (lambda bi, i: (0, 0, 0))

    def yblk():
        return pl.BlockSpec((1, tm, 512), lambda bi, i: (bi, i, 0))

    def gblk(col):
        return pl.BlockSpec((1, tm, D_MODEL), lambda bi, i, col=col: (bi, i, col))

    def wblk(k):
        return pl.BlockSpec((k, D_MODEL), lambda bi, i: (0, 0))

    return pl.pallas_call(
        _merge_kernel,
        grid=(b, t // tm),
        in_specs=[pl.BlockSpec((1, tm, D_MODEL), lambda bi, i: (bi, i, 0)), yblk(), yblk(), yblk(),
                  gblk(5), gblk(6), gblk(7), pl.BlockSpec((1, 1, D_MODEL), mod_map),
                  wblk(512), wblk(512), wblk(512), wblk(D_MODEL)],
        out_specs=pl.BlockSpec((1, tm, D_MODEL), lambda bi, i: (bi, i, 0)),
        out_shape=jax.ShapeDtypeStruct(x.shape, F32),
        compiler_params=_cparams(("parallel", "parallel")),
        name="merge",
    )(x, y_m, y_a, y_s, u, u, u, mod2, w_m, w_a, w_s, w_o)


def _ffn_kernel(seq, x_ref, sh_ref, sc_ref, gt_ref, g_ref, wa_ref, wg_ref, cwa_ref, cwg_ref, wd_ref, o_ref, h_s):
    j = pl.program_id(1)
    last = pl.num_programs(1) - 1

    @pl.when(j == 0)
    def _():
        h = _rms(x_ref[0], g_ref[...]) * (1.0 + sc_ref[0]) + sh_ref[0]
        h_s[...] = h.astype(BF16)
        o_ref[...] = jnp.zeros_like(o_ref)

    a = _dwconv3(jnp.dot(h_s[...], wa_ref[...], preferred_element_type=F32), cwa_ref[...], period=seq)
    g = _dwconv3(jnp.dot(h_s[...], wg_ref[...], preferred_element_type=F32), cwg_ref[...], period=seq)
    o_ref[0] += _dot(_silu(g) * a, wd_ref[...])

    @pl.when(j == last)
    def _():
        o_ref[0] = x_ref[0] + gt_ref[0] * o_ref[0]


def _ffn(x, shift, scale, gate, g, w_up, conv_w, w_down, seq=None):
    b, t, _ = x.shape
    tf = 256
    nf = D_FF // tf
    per_batch = shift.shape[0] > 1
    mod_map = (lambda bi, j: (bi, 0, 0)) if per_batch else (lambda bi, j: (0, 0, 0))
    mod_spec = pl.BlockSpec((1, 1, D_MODEL), mod_map)
    return pl.pallas_call(
        functools.partial(_ffn_kernel, t if seq is None else seq),
        grid=(b, nf),
        in_specs=[pl.BlockSpec((1, t, D_MODEL), lambda bi, j: (bi, 0, 0)), mod_spec, mod_spec, mod_spec,
                  pl.BlockSpec((1, D_MODEL), lambda bi, j: (0, 0)),
                  pl.BlockSpec((D_MODEL, tf), lambda bi, j: (0, j)),
                  pl.BlockSpec((D_MODEL, tf), lambda bi, j: (0, nf + j)),
                  pl.BlockSpec((3, tf), lambda bi, j: (0, j)),
                  pl.BlockSpec((3, tf), lambda bi, j: (0, nf + j)),
                  pl.BlockSpec((tf, D_MODEL), lambda bi, j: (j, 0))],
        out_specs=pl.BlockSpec((1, t, D_MODEL), lambda bi, j: (bi, 0, 0)),
        out_shape=jax.ShapeDtypeStruct(x.shape, F32),
        scratch_shapes=[pltpu.VMEM((t, D_MODEL), BF16)],
        compiler_params=_cparams(("parallel", "arbitrary")),
        name="conv_ffn",
    )(x, shift, scale, gate, g, w_up, w_up, conv_w, conv_w, w_down)


def _rope_tables(t):
    pos = jnp.arange(t)
    inv_freq = ROPE_BASE ** (-jnp.arange(ROPE_FREQS, dtype=F32) / ROPE_FREQS)
    ang = jnp.stack([pos // GRID_W, pos % GRID_W], axis=-1).astype(F32)[..., None] * inv_freq
    cos, sin = jnp.cos(ang), jnp.sin(ang)
    cos64 = jnp.concatenate([cos[:, 0], cos[:, 0], cos[:, 1], cos[:, 1]], axis=-1)
    sin64 = jnp.concatenate([-sin[:, 0], sin[:, 0], -sin[:, 1], sin[:, 1]], axis=-1)
    return jnp.tile(cos64, (1, 2)), jnp.tile(sin64, (1, 2))


def _gate_lanes():
    idx = [-1] * GATE_W
    for q in range(4):
        for d in range(2):
            for h in range(M_HEADS):
                idx[q * 8 + d * 4 + h] = d * M_HEADS + h
                idx[LANES + q * 8 + d * 4 + h] = 2 * M_HEADS + d * M_HEADS + h
    for d in range(2):
        for h in range(S_HEADS):
            idx[DT_LANE0 + d * S_HEADS + h] = 4 * M_HEADS + d * S_HEADS + h
    return idx


def _gate_row(i_vals, f_vals, dt_vals):
    idx = jnp.array(_gate_lanes())
    src = jnp.concatenate([i_vals.reshape(-1), f_vals.reshape(-1), dt_vals.reshape(-1)])
    return jnp.where(idx >= 0, src[jnp.maximum(idx, 0)], 0.0)[None, :]


def _gate_weights(wl):
    idx = jnp.array(_gate_lanes())
    mg = wl[:, _MG0:_MG1].reshape(D_MODEL, 2, 2, M_HEADS)
    src = jnp.concatenate([mg[:, :, 0].reshape(D_MODEL, -1), mg[:, :, 1].reshape(D_MODEL, -1),
                           wl[:, _DT0:_DT1]], axis=1)
    return jnp.where(idx >= 0, src[:, jnp.maximum(idx, 0)], 0.0)


def kernel(x, c, ctx, c_ctx, w_mod, b_mod, norm1_g, norm2_g, w_in, m_conv_w, m_conv_b, m_igate_b, m_fgate_b, m_norm_g, a_qnorm_g, a_knorm_g, a_lambda, a_subln_g, s_conv_w, s_conv_b, s_dt_bias, s_a_log, s_d, s_norm_g, w_branch_m, w_branch_a, w_branch_s, w_out, w_up, ffn_conv_w, w_down):
    b, t, _ = x.shape
    cos, sin = _rope_tables(t)

    c_all = jnp.concatenate([c, c_ctx[None, :], jnp.zeros((7, D_MODEL), F32)], axis=0)
    mod_all = _modulation(c_all, w_mod, b_mod)

    xc = ctx
    for l in range(DEPTH):
        emit = l < DEPTH - 1
        lam_init = 0.8 - 0.6 * math.exp(-0.3 * l)
        mod = [mod_all[l, :b, i * D_MODEL:(i + 1) * D_MODEL][:, None, :] for i in range(N_MOD)]
        modc = [mod_all[l, b:b + 1, i * D_MODEL:(i + 1) * D_MODEL][:, None, :] for i in range(N_MOD)]

        wl = w_in[l]
        w_main = jnp.concatenate([wl[:, :_MG0], wl[:, _MG1:_DT0], wl[:, _DT1:]], axis=1).astype(BF16)
        w_small = _gate_weights(wl).astype(BF16)
        zeros8 = jnp.zeros((2 * M_HEADS,), F32)
        gate_bias = _gate_row(m_igate_b[l], m_fgate_b[l], s_dt_bias[l])
        alog_row = _gate_row(zeros8, zeros8, s_a_log[l])[:, :LANES]
        dskip_row = jnp.repeat(s_d[l], S_HEAD_DIM)[None, :]
        qg = jnp.tile(a_qnorm_g[l], 2)[None, :]
        kg = jnp.tile(a_knorm_g[l], 2)[None, :]

        n1 = norm1_g[l][None, :]
        u, ug = _inproj(x, mod[0], mod[1], n1, w_main, w_small)
        tc = xc.shape[1]
        uc, ugc = _inproj(xc.reshape(1, b * tc, D_MODEL), modc[0], modc[1], n1, w_main, w_small)
        uc, ugc = uc.reshape(b, tc, U_MAIN), ugc.reshape(b, tc, GATE_W)

        y_m, y_mc = _mlstm(u, ug, uc, ugc, m_conv_w[l], m_conv_b[l][None, :], gate_bias,
                           m_norm_g[l][None, :], emit)
        y_s, y_sc = _ssd(u, ug, uc, ugc, s_conv_w[l], s_conv_b[l][None, :], gate_bias, alog_row,
                         dskip_row, s_norm_g[l][None, :], emit)
        y_a = _attention(u, [(u, True), (uc, False)], cos, sin, qg, kg, a_lambda[l],
                         a_subln_g[l][None, :], lam_init, True)

        wm, wa, ws, wo = (w_branch_m[l].astype(BF16), w_branch_a[l].astype(BF16),
                          w_branch_s[l].astype(BF16), w_out[l].astype(BF16))
        wu, wd = w_up[l].astype(BF16), w_down[l].astype(BF16)
        n2 = norm2_g[l][None, :]
        x = _merge(x, y_m, y_a, y_s, u, mod[2], wm, wa, ws, wo)
        x = _ffn(x, mod[3], mod[4], mod[5], n2, wu, ffn_conv_w[l], wd)

        if emit:
            y_ac = _attention(uc, [(uc, False)], cos, sin, qg, kg, a_lambda[l],
                              a_subln_g[l][None, :], lam_init, False)
            flat = lambda a: a.reshape(1, b * tc, a.shape[-1])
            xc = _merge(flat(xc), flat(y_mc), flat(y_ac), flat(y_sc), flat(uc), modc[2], wm, wa, ws, wo)
            pack = math.gcd(b, max(1, 2048 // tc))
            xc = _ffn(xc.reshape(b // pack, pack * tc, D_MODEL), modc[3], modc[4], modc[5], n2, wu,
                      ffn_conv_w[l], wd, seq=tc).reshape(b, tc, D_MODEL)
    return x
```

```python
import functools
import math

import jax
import jax.numpy as jnp
from jax import lax
from jax.experimental import pallas as pl
from jax.experimental.pallas import tpu as pltpu

F32 = jnp.float32
BF16 = jnp.bfloat16

D_MODEL = 1024
DEPTH = 2
GRID_W = 64
M_HEADS = 4
M_HEAD_DIM = 128
M_WIDTH = 512
A_HEADS = 4
A_SUB_DIM = 64
A_V_DIM = 128
A_WIDTH = 512
ROPE_BASE = 10000.0
ROPE_FREQS = 16
S_HEADS = 8
S_HEAD_DIM = 64
S_WIDTH = 512
S_GROUPS = 2
S_STATE = 128
CHUNK = 64
D_FF = 2816
N_MOD = 6
NORM_EPS = 1e-6
LOG2E = 1.4426950408889634
LANES = 128
U_MAIN = 8192
GATE_W = 2 * LANES
DT_LANE0 = 64
CT_ROWS = M_HEAD_DIM + 16
_MG0, _MG1 = 2048, 2064
_DT0, _DT1 = 5136, 5152
VMEM_LIMIT = 56 * 1024 * 1024


def _cparams(sem):
    return pltpu.CompilerParams(dimension_semantics=sem, vmem_limit_bytes=VMEM_LIMIT)


def _sigmoid(x):
    return 1.0 / (1.0 + jnp.exp(-x))


def _silu(x):
    return x * _sigmoid(x)


def _softplus(x):
    return jnp.maximum(x, 0.0) + jnp.log1p(jnp.exp(-jnp.abs(x)))


def _dot(a, b):
    return jnp.dot(a.astype(BF16), b.astype(BF16), preferred_element_type=F32)


def _dot_nt(a, b):
    return lax.dot_general(a.astype(BF16), b.astype(BF16), (((1,), (1,)), ((), ())),
                           preferred_element_type=F32)


def _dwconv3(x, w, b=None, period=None):
    t = x.shape[0]
    row = lax.broadcasted_iota(jnp.int32, x.shape, 0)
    if period is not None and period < t:
        row = row % period
        last = period - 1
    else:
        last = t - 1
    prev = jnp.where(row == 0, 0.0, pltpu.roll(x, 1, 0))
    nxt = jnp.where(row == last, 0.0, pltpu.roll(x, t - 1, 0))
    y = prev * w[0:1] + x * w[1:2] + nxt * w[2:3]
    if b is not None:
        y = y + b
    return y


def _cumsum_rows(x, reverse):
    n = x.shape[0]
    row = lax.broadcasted_iota(jnp.int32, x.shape, 0)
    k = 1
    while k < n:
        if reverse:
            x = x + jnp.where(row < n - k, pltpu.roll(x, n - k, 0), 0.0)
        else:
            x = x + jnp.where(row >= k, pltpu.roll(x, k, 0), 0.0)
        k *= 2
    return x


def _rms(x, g):
    return x * lax.rsqrt(jnp.mean(x * x, axis=-1, keepdims=True) + NORM_EPS) * g


def _mod_kernel(c_ref, w_ref, b_ref, o_ref):
    o_ref[0] = _dot(_silu(c_ref[...]), w_ref[0]) + b_ref[0]


def _modulation(c_all, w_mod, b_mod):
    rows = c_all.shape[0]
    tn = 1536
    return pl.pallas_call(
        _mod_kernel,
        grid=(DEPTH, N_MOD * D_MODEL // tn),
        in_specs=[pl.BlockSpec((rows, D_MODEL), lambda l, j: (0, 0)),
                  pl.BlockSpec((1, D_MODEL, tn), lambda l, j: (l, 0, j)),
                  pl.BlockSpec((1, 1, tn), lambda l, j: (l, 0, j))],
        out_specs=pl.BlockSpec((1, rows, tn), lambda l, j: (l, 0, j)),
        out_shape=jax.ShapeDtypeStruct((DEPTH, rows, N_MOD * D_MODEL), F32),
        compiler_params=_cparams(("arbitrary", "arbitrary")),
        name="modulation",
    )(c_all, w_mod.astype(BF16), b_mod[:, None, :])


def _inproj_kernel(x_ref, sh_ref, sc_ref, g_ref, w_ref, ws_ref, u_ref, ug_ref, h_scr):
    @pl.when(pl.program_id(2) == 0)
    def _():
        h = _rms(x_ref[0], g_ref[...]) * (1.0 + sc_ref[0]) + sh_ref[0]
        hb = h.astype(BF16)
        h_scr[...] = hb
        ug_ref[0] = jnp.dot(hb, ws_ref[...], preferred_element_type=F32)

    u_ref[0] = jnp.dot(h_scr[...], w_ref[...], preferred_element_type=F32).astype(u_ref.dtype)


def _inproj(x, shift, scale, g, w_main, w_small):
    b, t, _ = x.shape
    tm = min(t, 1024)
    tn = 2048
    per_batch = shift.shape[0] > 1
    mod_map = (lambda bi, i, j: (bi, 0, 0)) if per_batch else (lambda bi, i, j: (0, 0, 0))
    return pl.pallas_call(
        _inproj_kernel,
        grid=(b, t // tm, U_MAIN // tn),
        in_specs=[pl.BlockSpec((1, tm, D_MODEL), lambda bi, i, j: (bi, i, 0)),
                  pl.BlockSpec((1, 1, D_MODEL), mod_map),
                  pl.BlockSpec((1, 1, D_MODEL), mod_map),
                  pl.BlockSpec((1, D_MODEL), lambda bi, i, j: (0, 0)),
                  pl.BlockSpec((D_MODEL, tn), lambda bi, i, j: (0, j)),
                  pl.BlockSpec((D_MODEL, GATE_W), lambda bi, i, j: (0, 0))],
        out_specs=[pl.BlockSpec((1, tm, tn), lambda bi, i, j: (bi, i, j)),
                   pl.BlockSpec((1, tm, GATE_W), lambda bi, i, j: (bi, i, 0))],
        out_shape=[jax.ShapeDtypeStruct((b, t, U_MAIN), BF16),
                   jax.ShapeDtypeStruct((b, t, GATE_W), F32)],
        scratch_shapes=[pltpu.VMEM((tm, D_MODEL), BF16)],
        compiler_params=_cparams(("parallel", "parallel", "arbitrary")),
        name="inproj",
    )(x, shift, scale, g, w_main, w_small)


def _cummax_rows(x, reverse):
    n = x.shape[0]
    row = lax.broadcasted_iota(jnp.int32, x.shape, 0)
    k = 1
    while k < n:
        if reverse:
            x = jnp.maximum(x, jnp.where(row < n - k, pltpu.roll(x, n - k, 0), -jnp.inf))
        else:
            x = jnp.maximum(x, jnp.where(row >= k, pltpu.roll(x, k, 0), -jnp.inf))
        k *= 2
    return x


def _mlstm_gates(d, gi, gf, m_old):
    tl = CHUNK - 1 if d == 0 else 0
    rev = d == 1
    b = _cumsum_rows(-_softplus(-gf), rev)
    ib = gi - b
    m_t = b + jnp.maximum(m_old, _cummax_rows(ib, rev))
    m_new = m_t[tl:tl + 1, :]
    b_last = b[tl:tl + 1, :]
    grp = lax.broadcasted_iota(jnp.int32, gi.shape, 1) // 8
    p = jnp.where(grp == 0, b - m_t,
                  jnp.where(grp == 1, jnp.exp(b + m_old - m_t),
                            jnp.where(grp == 2, jnp.exp(-m_t), jnp.exp(b_last - m_new + ib))))
    pt = jnp.concatenate([p, pltpu.roll(p, LANES - 1, 1)], axis=0).T
    decay = jnp.exp(b_last + m_old - m_new)
    used = lax.broadcasted_iota(jnp.int32, m_new.shape, 1) < 32
    return pt, ib, decay, jnp.where(used, m_new, 0.0)


def _mlstm_pair(d, p, emit, kp, qt, vt, pt, ib, decay, ct_s):
    def row(q):
        r = q * 8 + d * 4 + 2 * p
        return pt[r:r + 1, :]

    la, lb = d * 4 + 2 * p, d * 4 + 2 * p + 1
    lane2 = lax.broadcasted_iota(jnp.int32, (1, 2 * LANES), 1)
    decay_l = jnp.where(lane2 < LANES, decay[:, la:la + 1], decay[:, lb:lb + 1])
    klane = lax.broadcasted_iota(jnp.int32, kp.shape, 1)
    zero = jnp.zeros_like(kp)
    kbd = jnp.concatenate([jnp.where(klane < LANES, kp, zero), jnp.where(klane < LANES, zero, kp)], axis=0)
    ct_old = ct_s[d, p]
    ct_s[d, p] = decay_l * ct_old + jnp.dot((vt.astype(F32) * row(3)).astype(BF16), kbd,
                                            preferred_element_type=F32)
    if not emit:
        return None
    res = jnp.dot(jnp.concatenate([kp, ct_old.astype(BF16)], axis=0), qt, preferred_element_type=F32)
    st, cq = res[0:CHUNK], res[CHUNK:]
    r = lax.broadcasted_iota(jnp.int32, (CHUNK, LANES), 0)
    lane = lax.broadcasted_iota(jnp.int32, (CHUNK, LANES), 1)
    tt = lane % CHUNK
    mask = (r <= tt) if d == 0 else (r >= tt)
    ibp = jnp.where(lane < CHUNK, ib[:, la:la + 1], ib[:, lb:lb + 1])
    wt = jnp.where(mask, jnp.exp(row(0) + ibp), 0.0) * st
    wbd = jnp.concatenate([jnp.where(lane < CHUNK, wt, 0.0), jnp.where(lane < CHUNK, 0.0, wt)], axis=0)
    num = row(1) * cq + jnp.dot(vt, wbd.astype(BF16), preferred_element_type=F32)
    den = num[M_HEAD_DIM:M_HEAD_DIM + 1, :]
    return num[0:M_HEAD_DIM] / jnp.maximum(jnp.abs(den), row(2))


def _mlstm_kernel(emit_ctx,
                  q_ref, k_ref, v_ref, o_ref, g_ref, qc_ref, kc_ref, vc_ref, oc_ref, gc_ref,
                  cwq_ref, cwk_ref, cbq_ref, cbk_ref, gb_ref, ng_ref,
                  y_ref, yc_ref,
                  q_s, k_s, qc_s, kc_s, qt_s, vt_s, acct_s, ct_s, m_s):
    t_lat, t_ctx = q_ref.shape[1], qc_ref.shape[1]
    nc_ctx, nc_lat = t_ctx // CHUNK, t_lat // CHUNK
    kscale = M_HEAD_DIM ** -0.5

    for h in range(M_HEADS):
        sl = slice(h * LANES, (h + 1) * LANES)
        for src, dst, cw, cb, sc in ((q_ref, q_s, cwq_ref, cbq_ref, 1.0), (k_ref, k_s, cwk_ref, cbk_ref, kscale),
                                     (qc_ref, qc_s, cwq_ref, cbq_ref, 1.0), (kc_ref, kc_s, cwk_ref, cbk_ref, kscale)):
            y = _silu(_dwconv3(src[0, :, sl].astype(F32), cw[:, sl], cb[:, sl]))
            dst[:, sl] = (y * sc).astype(BF16) if sc != 1.0 else y.astype(BF16)

    def transposes(nchunks, base, qs, vr):
        def body(c, carry):
            rows = pl.ds(pl.multiple_of(c * CHUNK, CHUNK), CHUNK)
            for p in range(M_HEADS // 2):
                sl2 = slice(p * 2 * LANES, (p + 1) * 2 * LANES)
                qp = qs[rows, sl2].astype(F32)
                lane = lax.broadcasted_iota(jnp.int32, qp.shape, 1)
                qbd = jnp.concatenate([jnp.where(lane < LANES, qp, 0.0), jnp.where(lane < LANES, 0.0, qp)], axis=0)
                qt_s[base + c, p] = qbd.T.astype(BF16)
                vp = vr[0, rows, sl2].astype(F32)
                vt_s[base + c, p, 0:M_HEAD_DIM, :] = jnp.concatenate([vp[:, 0:LANES], vp[:, LANES:]],
                                                                     axis=0).T.astype(BF16)
                vt_s[base + c, p, M_HEAD_DIM:, :] = jnp.ones((CT_ROWS - M_HEAD_DIM, LANES), BF16)
                acct_s[base + c, p] = jnp.zeros((M_HEAD_DIM, LANES), F32)
            return carry
        lax.fori_loop(0, nchunks, body, 0, unroll=8 if nchunks >= 16 else 2)

    transposes(nc_ctx, 0, qc_s, vc_ref)
    transposes(nc_lat, nc_ctx, q_s, v_ref)
    ct_s[...] = jnp.zeros_like(ct_s)
    m_s[...] = jnp.zeros_like(m_s)

    def run(nchunks, base, emit, ks, gr):
        def body(j, carry):
            for d in range(2):
                c = j if d == 0 else nchunks - 1 - j
                rows = pl.ds(pl.multiple_of(c * CHUNK, CHUNK), CHUNK)
                pt, ib, decay, m_new = _mlstm_gates(d, gr[0, rows, 0:LANES] + gb_ref[:, 0:LANES],
                                                    gr[0, rows, LANES:] + gb_ref[:, LANES:], m_s[d])
                m_s[d] = m_new
                for p in range(M_HEADS // 2):
                    out = _mlstm_pair(d, p, emit, ks[rows, p * 2 * LANES:(p + 1) * 2 * LANES],
                                      qt_s[base + c, p], vt_s[base + c, p], pt, ib, decay, ct_s)
                    if emit:
                        acct_s[base + c, p] += out
            return carry
        lax.fori_loop(0, nchunks, body, 0, unroll=8 if nchunks >= 16 else 2)

    run(nc_ctx, 0, emit_ctx, kc_s, gc_ref)
    run(nc_lat, nc_ctx, True, k_s, g_ref)

    def finish(nchunks, base, oref, yref):
        def body(c, carry):
            rows = pl.ds(pl.multiple_of(c * CHUNK, CHUNK), CHUNK)
            for p in range(M_HEADS // 2):
                ht = acct_s[base + c, p].T
                for hh in range(2):
                    sl = slice((2 * p + hh) * LANES, (2 * p + hh + 1) * LANES)
                    hn = _rms(ht[hh * CHUNK:(hh + 1) * CHUNK], ng_ref[:, sl])
                    yref[0, rows, sl] = (_sigmoid(oref[0, rows, sl].astype(F32)) * hn).astype(yref.dtype)
            return carry
        lax.fori_loop(0, nchunks, body, 0, unroll=8 if nchunks >= 16 else 2)

    finish(nc_lat, nc_ctx, o_ref, y_ref)
    if emit_ctx:
        finish(nc_ctx, 0, oc_ref, yc_ref)
    else:
        yc_ref[...] = jnp.zeros_like(yc_ref)


def _mlstm(u, ug, uc, ugc, conv_w, conv_b, gate_bias, norm_g, emit_ctx):
    b, t, _ = u.shape
    tc = uc.shape[1]
    w = M_WIDTH
    nchunks = (t + tc) // CHUNK
    pairs = M_HEADS // 2

    def ublk(tt, col):
        return pl.BlockSpec((1, tt, w), lambda bi, col=col: (bi, 0, col))

    def full(shape):
        return pl.BlockSpec(shape, lambda bi: (0,) * len(shape))

    return pl.pallas_call(
        functools.partial(_mlstm_kernel, emit_ctx),
        grid=(b,),
        in_specs=[ublk(t, 0), ublk(t, 1), ublk(t, 2), ublk(t, 3),
                  pl.BlockSpec((1, t, 2 * LANES), lambda bi: (bi, 0, 0)),
                  ublk(tc, 0), ublk(tc, 1), ublk(tc, 2), ublk(tc, 3),
                  pl.BlockSpec((1, tc, 2 * LANES), lambda bi: (bi, 0, 0)),
                  pl.BlockSpec((3, w), lambda bi: (0, 0)), pl.BlockSpec((3, w), lambda bi: (0, 1)),
                  pl.BlockSpec((1, w), lambda bi: (0, 0)), pl.BlockSpec((1, w), lambda bi: (0, 1)),
                  full((1, 2 * LANES)), full((1, w))],
        out_specs=[pl.BlockSpec((1, t, w), lambda bi: (bi, 0, 0)),
                   pl.BlockSpec((1, tc, w), lambda bi: (bi, 0, 0))],
        out_shape=[jax.ShapeDtypeStruct((b, t, w), BF16), jax.ShapeDtypeStruct((b, tc, w), BF16)],
        scratch_shapes=[pltpu.VMEM((t, w), BF16), pltpu.VMEM((t, w), BF16),
                        pltpu.VMEM((tc, w), BF16), pltpu.VMEM((tc, w), BF16),
                        pltpu.VMEM((nchunks, pairs, 2 * M_HEAD_DIM, LANES), BF16),
                        pltpu.VMEM((nchunks, pairs, CT_ROWS, LANES), BF16),
                        pltpu.VMEM((nchunks, pairs, M_HEAD_DIM, LANES), F32),
                        pltpu.VMEM((2, pairs, CT_ROWS, 2 * LANES), F32),
                        pltpu.VMEM((2, 1, LANES), F32)],
        compiler_params=_cparams(("parallel",)),
        name="mlstm",
    )(u, u, u, u, ug, uc, uc, uc, uc, ugc, conv_w, conv_w, conv_b, conv_b, gate_bias, norm_g)


def _head_lanes(cols):
    rows = cols[0].shape[0]
    lane = lax.broadcasted_iota(jnp.int32, (rows, 4 * S_HEAD_DIM), 1) // S_HEAD_DIM
    out = jnp.broadcast_to(cols[3], lane.shape)
    for hh in (2, 1, 0):
        out = jnp.where(lane == hh, cols[hh], out)
    return out


def _ssd_stream(d, g, emit, xg, bg, cg, dtt, st, stt, st_s):
    tl = CHUNK - 1 if d == 0 else 0
    c0 = DT_LANE0 + d * S_HEADS + g * 4
    gw = 4 * S_HEAD_DIM
    scols = [st[:, c0 + hh:c0 + hh + 1] for hh in range(4)]
    s_l = _head_lanes(scols)
    dt_l = _head_lanes([dtt[:, c0 + hh:c0 + hh + 1] for hh in range(4)])
    s_last = s_l[tl:tl + 1, :]
    xdt = xg * dt_l
    bt4 = jnp.concatenate([bg] * 4, axis=0).T
    bt = bt4[:, 0:CHUNK]
    st_old = st_s[d, g]
    st_s[d, g] = jnp.exp(s_last) * st_old + _dot(bt, xdt * jnp.exp(s_last - s_l))
    if not emit:
        return None
    srow = jnp.concatenate([stt[c0 + hh:c0 + hh + 1, :] for hh in range(4)], axis=1)
    r = lax.broadcasted_iota(jnp.int32, (CHUNK, gw), 0)
    lane = lax.broadcasted_iota(jnp.int32, (CHUNK, gw), 1)
    sp = lane % S_HEAD_DIM
    mask = (sp <= r) if d == 0 else (sp >= r)
    seg = jnp.where(mask, jnp.exp(s_l - srow), 0.0)
    m_all = _dot(cg, bt4) * seg
    rr = lax.broadcasted_iota(jnp.int32, (gw, gw), 0) // S_HEAD_DIM
    ll = lax.broadcasted_iota(jnp.int32, (gw, gw), 1) // S_HEAD_DIM
    xd = jnp.where(rr == ll, jnp.concatenate([xdt] * 4, axis=0), 0.0)
    return _dot(m_all, xd) + jnp.exp(s_l) * _dot(cg, st_old)


def _ssd_kernel(emit_ctx,
                x_ref, bc_ref, z_ref, g_ref, xc_ref, bcc_ref, zc_ref, gc_ref,
                cwx_ref, cwb_ref, cbx_ref, cbb_ref, gb_ref, al_ref, dsk_ref, ng_ref,
                y_ref, yc_ref,
                x_s, bc_s, xc_s, bcc_s, acc_s, accc_s, st_s):
    t_lat, t_ctx = x_ref.shape[1], xc_ref.shape[1]
    gw = 4 * S_HEAD_DIM

    for c in range(S_WIDTH // LANES):
        sl = slice(c * LANES, (c + 1) * LANES)
        for src, dst, cw, cb in ((x_ref, x_s, cwx_ref, cbx_ref), (bc_ref, bc_s, cwb_ref, cbb_ref),
                                 (xc_ref, xc_s, cwx_ref, cbx_ref), (bcc_ref, bcc_s, cwb_ref, cbb_ref)):
            dst[:, sl] = _silu(_dwconv3(src[0, :, sl].astype(F32), cw[:, sl], cb[:, sl])).astype(dst.dtype)

    acc_s[...] = jnp.zeros_like(acc_s)
    accc_s[...] = jnp.zeros_like(accc_s)
    st_s[...] = jnp.zeros_like(st_s)
    acoef = -jnp.exp(al_ref[...])

    def run(nchunks, emit, xs, bcs, gr, acc):
        def body(j, carry):
            for d in range(2):
                c = j if d == 0 else nchunks - 1 - j
                rows = pl.ds(pl.multiple_of(c * CHUNK, CHUNK), CHUNK)
                dtt = _softplus(gr[0, rows, :] + gb_ref[...])
                st = _cumsum_rows(dtt * acoef, reverse=(d == 1))
                stt = st.T
                for g in range(S_GROUPS):
                    out = _ssd_stream(d, g, emit, xs[rows, g * gw:(g + 1) * gw],
                                      bcs[rows, g * S_STATE:(g + 1) * S_STATE],
                                      bcs[rows, (S_GROUPS + g) * S_STATE:(S_GROUPS + g + 1) * S_STATE],
                                      dtt, st, stt, st_s)
                    if emit:
                        acc[rows, g * gw:(g + 1) * gw] += out
            return carry
        lax.fori_loop(0, nchunks, body, 0, unroll=8 if nchunks >= 16 else 2)

    run(t_ctx // CHUNK, emit_ctx, xc_s, bcc_s, gc_ref, accc_s)
    run(t_lat // CHUNK, True, x_s, bc_s, g_ref, acc_s)

    def finish(acc, xs, zref, yref):
        for g in range(S_GROUPS):
            sl = slice(g * gw, (g + 1) * gw)
            yv = (acc[:, sl] + dsk_ref[:, sl] * xs[:, sl]) * _silu(zref[0, :, sl].astype(F32))
            yref[0, :, sl] = _rms(yv, ng_ref[:, sl]).astype(yref.dtype)

    finish(acc_s, x_s, z_ref, y_ref)
    if emit_ctx:
        finish(accc_s, xc_s, zc_ref, yc_ref)
    else:
        yc_ref[...] = jnp.zeros_like(yc_ref)


def _ssd(u, ug, uc, ugc, conv_w, conv_b, gate_bias, alog_row, dskip_row, norm_g, emit_ctx):
    b, t, _ = u.shape
    tc = uc.shape[1]
    w = S_WIDTH

    def ublk(tt, col):
        return pl.BlockSpec((1, tt, w), lambda bi, col=col: (bi, 0, col))

    def full(shape):
        return pl.BlockSpec(shape, lambda bi: (0,) * len(shape))

    return pl.pallas_call(
        functools.partial(_ssd_kernel, emit_ctx),
        grid=(b,),
        in_specs=[ublk(t, 8), ublk(t, 9), ublk(t, 7), pl.BlockSpec((1, t, LANES), lambda bi: (bi, 0, 0)),
                  ublk(tc, 8), ublk(tc, 9), ublk(tc, 7), pl.BlockSpec((1, tc, LANES), lambda bi: (bi, 0, 0)),
                  pl.BlockSpec((3, w), lambda bi: (0, 0)), pl.BlockSpec((3, w), lambda bi: (0, 1)),
                  pl.BlockSpec((1, w), lambda bi: (0, 0)), pl.BlockSpec((1, w), lambda bi: (0, 1)),
                  full((1, LANES)), full((1, LANES)), full((1, w)), full((1, w))],
        out_specs=[pl.BlockSpec((1, t, w), lambda bi: (bi, 0, 0)),
                   pl.BlockSpec((1, tc, w), lambda bi: (bi, 0, 0))],
        out_shape=[jax.ShapeDtypeStruct((b, t, w), BF16), jax.ShapeDtypeStruct((b, tc, w), BF16)],
        scratch_shapes=[pltpu.VMEM((t, w), F32), pltpu.VMEM((t, w), F32),
                        pltpu.VMEM((tc, w), F32), pltpu.VMEM((tc, w), F32),
                        pltpu.VMEM((t, w), F32), pltpu.VMEM((tc, w), F32),
                        pltpu.VMEM((2, S_GROUPS, S_STATE, 4 * S_HEAD_DIM), F32)],
        compiler_params=_cparams(("parallel",)),
        name="ssd",
    )(u, u, u, ug, uc, uc, uc, ugc, conv_w, conv_w, conv_b, conv_b, gate_bias, alog_row, dskip_row, norm_g)


def _qk_prep(x, g, cos, sin):
    lane = lax.broadcasted_iota(jnp.int32, x.shape, 1)
    lo = lane < A_SUB_DIM
    x2 = x * x
    s_lo = jnp.sum(jnp.where(lo, x2, 0.0), axis=-1, keepdims=True)
    s_hi = jnp.sum(jnp.where(lo, 0.0, x2), axis=-1, keepdims=True)
    r_lo = lax.rsqrt(s_lo * (1.0 / A_SUB_DIM) + NORM_EPS)
    r_hi = lax.rsqrt(s_hi * (1.0 / A_SUB_DIM) + NORM_EPS)
    y = x * jnp.where(lo, r_lo, r_hi) * g
    if cos is not None:
        swapped = jnp.where(lane % (2 * ROPE_FREQS) < ROPE_FREQS,
                            pltpu.roll(y, LANES - ROPE_FREQS, 1), pltpu.roll(y, ROPE_FREQS, 1))
        y = y * cos + swapped * sin
    return y


def _attn_kernel(lam_init, rope_q, src_rope, tq, *refs):
    n_src = len(src_rope)
    q_ref = refs[0]
    kv_refs = refs[1:1 + 2 * n_src]
    cos_ref, sin_ref, qg_ref, kg_ref, lam_ref, sg_ref = refs[1 + 2 * n_src:7 + 2 * n_src]
    o_ref = refs[7 + 2 * n_src]
    k_s, v_s = refs[8 + 2 * n_src:]
    qi = pl.program_id(1)

    @pl.when(qi == 0)
    def _():
        off = 0
        for si in range(n_src):
            k_ref, v_ref = kv_refs[2 * si], kv_refs[2 * si + 1]
            ts = k_ref.shape[1]
            for h in range(A_HEADS):
                sl = slice(h * LANES, (h + 1) * LANES)
                cs = (cos_ref[0:ts, :], sin_ref[0:ts, :]) if src_rope[si] else (None, None)
                k_s[h, off:off + ts, :] = _qk_prep(k_ref[0, :, sl].astype(F32), kg_ref[...], *cs).astype(BF16)
                v_s[h, off:off + ts, 0:LANES] = v_ref[0, :, sl]
                v_s[h, off:off + ts, LANES:2 * LANES] = jnp.ones((ts, LANES), BF16)
            off += ts

    lp = lam_ref[...]
    lam = (jnp.exp(jnp.sum(lp[0:1] * lp[1:2], axis=-1, keepdims=True))
           - jnp.exp(jnp.sum(lp[2:3] * lp[3:4], axis=-1, keepdims=True)) + lam_init)
    if rope_q:
        rows = pl.ds(pl.multiple_of(qi * tq, tq), tq)
        cs = (cos_ref[rows, :], sin_ref[rows, :])
    else:
        cs = (None, None)
    for h in range(A_HEADS):
        sl = slice(h * LANES, (h + 1) * LANES)
        qh = _qk_prep(q_ref[0, :, sl].astype(F32), qg_ref[...], *cs) * (A_SUB_DIM ** -0.5 * LOG2E)
        lo = lax.broadcasted_iota(jnp.int32, qh.shape, 1) < A_SUB_DIM
        q2 = jnp.concatenate([jnp.where(lo, qh, 0.0), jnp.where(lo, 0.0, qh)], axis=0).astype(BF16)
        s = _dot_nt(q2, k_s[h])
        e = jnp.exp2(s - jnp.max(s, axis=-1, keepdims=True)).astype(BF16)
        pv = jnp.dot(e, v_s[h], preferred_element_type=F32)
        o = (pv[0:tq, 0:LANES] / pv[0:tq, LANES:2 * LANES]
             - lam * (pv[tq:2 * tq, 0:LANES] / pv[tq:2 * tq, LANES:2 * LANES]))
        o_ref[0, :, sl] = (_rms(o, sg_ref[...]) * (1.0 - lam_init)).astype(o_ref.dtype)


def _attention(uq, sources, cos, sin, qg, kg, lam_p, sub_g, lam_init, rope_q):
    b, t, _ = uq.shape
    tq = min(t, 256)
    w = A_WIDTH
    tk = sum(s[0].shape[1] for s in sources)
    in_specs = [pl.BlockSpec((1, tq, w), lambda bi, qi: (bi, qi, 4))]
    args = [uq]
    for arr, _ in sources:
        ts = arr.shape[1]
        in_specs += [pl.BlockSpec((1, ts, w), lambda bi, qi: (bi, 0, 5)),
                     pl.BlockSpec((1, ts, w), lambda bi, qi: (bi, 0, 6))]
        args += [arr, arr]
    tcs = cos.shape[0]
    for shape in ((tcs, LANES), (tcs, LANES), (1, LANES), (1, LANES), (4, A_SUB_DIM), (1, LANES)):
        in_specs.append(pl.BlockSpec(shape, lambda bi, qi: (0, 0)))
    args += [cos, sin, qg, kg, lam_p, sub_g]
    return pl.pallas_call(
        functools.partial(_attn_kernel, lam_init, rope_q, tuple(r for _, r in sources), tq),
        grid=(b, t // tq),
        in_specs=in_specs,
        out_specs=pl.BlockSpec((1, tq, w), lambda bi, qi: (bi, qi, 0)),
        out_shape=jax.ShapeDtypeStruct((b, t, w), BF16),
        scratch_shapes=[pltpu.VMEM((A_HEADS, tk, LANES), BF16), pltpu.VMEM((A_HEADS, tk, 2 * LANES), BF16)],
        compiler_params=_cparams(("parallel", "arbitrary")),
        name="diff_attention",
    )(*args)


def _merge_kernel(x_ref, ym_ref, ya_ref, ys_ref, gm_ref, ga_ref, gs_ref, m2_ref,
                  wm_ref, wa_ref, ws_ref, wo_ref, o_ref):
    t = (_sigmoid(gm_ref[0].astype(F32)) * _dot(ym_ref[0], wm_ref[...])
         + _sigmoid(ga_ref[0].astype(F32)) * _dot(ya_ref[0], wa_ref[...])
         + _sigmoid(gs_ref[0].astype(F32)) * _dot(ys_ref[0], ws_ref[...]))
    o_ref[0] = x_ref[0] + m2_ref[0] * _dot(t, wo_ref[...])


def _merge(x, y_m, y_a, y_s, u, mod2, w_m, w_a, w_s, w_o):
    b, t, _ = x.shape
    tm = min(t, 512)
    per_batch = mod2.shape[0] > 1
    mod_map = (lambda bi, i: (bi, 0, 0)) if per_batch else (lambda bi, i: (0, 0, 0))

    def yblk():
        return pl.BlockSpec((1, tm, 512), lambda bi, i: (bi, i, 0))

    def gblk(col):
        return pl.BlockSpec((1, tm, D_MODEL), lambda bi, i, col=col: (bi, i, col))

    def wblk(k):
        return pl.BlockSpec((k, D_MODEL), lambda bi, i: (0, 0))

    return pl.pallas_call(
        _merge_kernel,
        grid=(b, t // tm),
        in_specs=[pl.BlockSpec((1, tm, D_MODEL), lambda bi, i: (bi, i, 0)), yblk(), yblk(), yblk(),
                  gblk(5), gblk(6), gblk(7), pl.BlockSpec((1, 1, D_MODEL), mod_map),
                  wblk(512), wblk(512), wblk(512), wblk(D_MODEL)],
        out_specs=pl.BlockSpec((1, tm, D_MODEL), lambda bi, i: (bi, i, 0)),
        out_shape=jax.ShapeDtypeStruct(x.shape, F32),
        compiler_params=_cparams(("parallel", "parallel")),
        name="merge",
    )(x, y_m, y_a, y_s, u, u, u, mod2, w_m, w_a, w_s, w_o)


def _ffn_kernel(seq, x_ref, sh_ref, sc_ref, gt_ref, g_ref, wa_ref, wg_ref, cwa_ref, cwg_ref, wd_ref, o_ref, h_s):
    j = pl.program_id(1)
    last = pl.num_programs(1) - 1

    @pl.when(j == 0)
    def _():
        h = _rms(x_ref[0], g_ref[...]) * (1.0 + sc_ref[0]) + sh_ref[0]
        h_s[...] = h.astype(BF16)
        o_ref[...] = jnp.zeros_like(o_ref)

    a = _dwconv3(jnp.dot(h_s[...], wa_ref[...], preferred_element_type=F32), cwa_ref[...], period=seq)
    g = _dwconv3(jnp.dot(h_s[...], wg_ref[...], preferred_element_type=F32), cwg_ref[...], period=seq)
    o_ref[0] += _dot(_silu(g) * a, wd_ref[...])

    @pl.when(j == last)
    def _():
        o_ref[0] = x_ref[0] + gt_ref[0] * o_ref[0]


def _ffn(x, shift, scale, gate, g, w_up, conv_w, w_down, seq=None):
    b, t, _ = x.shape
    tf = 256
    nf = D_FF // tf
    per_batch = shift.shape[0] > 1
    mod_map = (lambda bi, j: (bi, 0, 0)) if per_batch else (lambda bi, j: (0, 0, 0))
    mod_spec = pl.BlockSpec((1, 1, D_MODEL), mod_map)
    return pl.pallas_call(
        functools.partial(_ffn_kernel, t if seq is None else seq),
        grid=(b, nf),
        in_specs=[pl.BlockSpec((1, t, D_MODEL), lambda bi, j: (bi, 0, 0)), mod_spec, mod_spec, mod_spec,
                  pl.BlockSpec((1, D_MODEL), lambda bi, j: (0, 0)),
                  pl.BlockSpec((D_MODEL, tf), lambda bi, j: (0, j)),
                  pl.BlockSpec((D_MODEL, tf), lambda bi, j: (0, nf + j)),
                  pl.BlockSpec((3, tf), lambda bi, j: (0, j)),
                  pl.BlockSpec((3, tf), lambda bi, j: (0, nf + j)),
                  pl.BlockSpec((tf, D_MODEL), lambda bi, j: (j, 0))],
        out_specs=pl.BlockSpec((1, t, D_MODEL), lambda bi, j: (bi, 0, 0)),
        out_shape=jax.ShapeDtypeStruct(x.shape, F32),
        scratch_shapes=[pltpu.VMEM((t, D_MODEL), BF16)],
        compiler_params=_cparams(("parallel", "arbitrary")),
        name="conv_ffn",
    )(x, shift, scale, gate, g, w_up, w_up, conv_w, conv_w, w_down)


def _rope_tables(t):
    pos = jnp.arange(t)
    inv_freq = ROPE_BASE ** (-jnp.arange(ROPE_FREQS, dtype=F32) / ROPE_FREQS)
    ang = jnp.stack([pos // GRID_W, pos % GRID_W], axis=-1).astype(F32)[..., None] * inv_freq
    cos, sin = jnp.cos(ang), jnp.sin(ang)
    cos64 = jnp.concatenate([cos[:, 0], cos[:, 0], cos[:, 1], cos[:, 1]], axis=-1)
    sin64 = jnp.concatenate([-sin[:, 0], sin[:, 0], -sin[:, 1], sin[:, 1]], axis=-1)
    return jnp.tile(cos64, (1, 2)), jnp.tile(sin64, (1, 2))


def _gate_lanes():
    idx = [-1] * GATE_W
    for q in range(4):
        for d in range(2):
            for h in range(M_HEADS):
                idx[q * 8 + d * 4 + h] = d * M_HEADS + h
                idx[LANES + q * 8 + d * 4 + h] = 2 * M_HEADS + d * M_HEADS + h
    for d in range(2):
        for h in range(S_HEADS):
            idx[DT_LANE0 + d * S_HEADS + h] = 4 * M_HEADS + d * S_HEADS + h
    return idx


def _gate_row(i_vals, f_vals, dt_vals):
    idx = jnp.array(_gate_lanes())
    src = jnp.concatenate([i_vals.reshape(-1), f_vals.reshape(-1), dt_vals.reshape(-1)])
    return jnp.where(idx >= 0, src[jnp.maximum(idx, 0)], 0.0)[None, :]


def _gate_weights(wl):
    idx = jnp.array(_gate_lanes())
    mg = wl[:, _MG0:_MG1].reshape(D_MODEL, 2, 2, M_HEADS)
    src = jnp.concatenate([mg[:, :, 0].reshape(D_MODEL, -1), mg[:, :, 1].reshape(D_MODEL, -1),
                           wl[:, _DT0:_DT1]], axis=1)
    return jnp.where(idx >= 0, src[:, jnp.maximum(idx, 0)], 0.0)


def kernel(x, c, ctx, c_ctx, w_mod, b_mod, norm1_g, norm2_g, w_in, m_conv_w, m_conv_b, m_igate_b, m_fgate_b, m_norm_g, a_qnorm_g, a_knorm_g, a_lambda, a_subln_g, s_conv_w, s_conv_b, s_dt_bias, s_a_log, s_d, s_norm_g, w_branch_m, w_branch_a, w_branch_s, w_out, w_up, ffn_conv_w, w_down):
    b, t, _ = x.shape
    cos, sin = _rope_tables(t)

    c_all = jnp.concatenate([c, c_ctx[None, :], jnp.zeros((7, D_MODEL), F32)], axis=0)
    mod_all = _modulation(c_all, w_mod, b_mod)

    xc = ctx
    for l in range(DEPTH):
        emit = l < DEPTH - 1
        lam_init = 0.8 - 0.6 * math.exp(-0.3 * l)
        mod = [mod_all[l, :b, i * D_MODEL:(i + 1) * D_MODEL][:, None, :] for i in range(N_MOD)]
        modc = [mod_all[l, b:b + 1, i * D_MODEL:(i + 1) * D_MODEL][:, None, :] for i in range(N_MOD)]

        wl = w_in[l]
        w_main = jnp.concatenate([wl[:, :_MG0], wl[:, _MG1:_DT0], wl[:, _DT1:]], axis=1).astype(BF16)
        w_small = _gate_weights(wl).astype(BF16)
        zeros8 = jnp.zeros((2 * M_HEADS,), F32)
        gate_bias = _gate_row(m_igate_b[l], m_fgate_b[l], s_dt_bias[l])
        alog_row = _gate_row(zeros8, zeros8, s_a_log[l])[:, :LANES]
        dskip_row = jnp.repeat(s_d[l], S_HEAD_DIM)[None, :]
        qg = jnp.tile(a_qnorm_g[l], 2)[None, :]
        kg = jnp.tile(a_knorm_g[l], 2)[None, :]

        n1 = norm1_g[l][None, :]
        u, ug = _inproj(x, mod[0], mod[1], n1, w_main, w_small)
        tc = xc.shape[1]
        uc, ugc = _inproj(xc.reshape(1, b * tc, D_MODEL), modc[0], modc[1], n1, w_main, w_small)
        uc, ugc = uc.reshape(b, tc, U_MAIN), ugc.reshape(b, tc, GATE_W)

        y_m, y_mc = _mlstm(u, ug, uc, ugc, m_conv_w[l], m_conv_b[l][None, :], gate_bias,
                           m_norm_g[l][None, :], emit)
        y_s, y_sc = _ssd(u, ug, uc, ugc, s_conv_w[l], s_conv_b[l][None, :], gate_bias, alog_row,
                         dskip_row, s_norm_g[l][None, :], emit)
        y_a = _attention(u, [(u, True), (uc, False)], cos, sin, qg, kg, a_lambda[l],
                         a_subln_g[l][None, :], lam_init, True)

        wm, wa, ws, wo = (w_branch_m[l].astype(BF16), w_branch_a[l].astype(BF16),
                          w_branch_s[l].astype(BF16), w_out[l].astype(BF16))
        wu, wd = w_up[l].astype(BF16), w_down[l].astype(BF16)
        n2 = norm2_g[l][None, :]
        x = _merge(x, y_m, y_a, y_s, u, mod[2], wm, wa, ws, wo)
        x = _ffn(x, mod[3], mod[4], mod[5], n2, wu, ffn_conv_w[l], wd)

        if emit:
            y_ac = _attention(uc, [(uc, False)], cos, sin, qg, kg, a_lambda[l],
                              a_subln_g[l][None, :], lam_init, False)
            flat = lambda a: a.reshape(1, b * tc, a.shape[-1])
            xc = _merge(flat(xc), flat(y_mc), flat(y_ac), flat(y_sc), flat(uc), modc[2], wm, wa, ws, wo)
            pack = math.gcd(b, max(1, 2048 // tc))
            xc = _ffn(xc.reshape(b // pack, pack * tc, D_MODEL), modc[3], modc[4], modc[5], n2, wu,
                      ffn_conv_w[l], wd, seq=tc).reshape(b, tc, D_MODEL)
    return x
```

```python
import functools
import math

import jax
import jax.numpy as jnp
from jax import lax
from jax.experimental import pallas as pl
from jax.experimental.pallas import tpu as pltpu

F32 = jnp.float32
BF16 = jnp.bfloat16

D_MODEL = 1024
DEPTH = 2
GRID_W = 64
M_HEADS = 4
M_HEAD_DIM = 128
M_WIDTH = 512
A_HEADS = 4
A_SUB_DIM = 64
A_V_DIM = 128
A_WIDTH = 512
ROPE_BASE = 10000.0
ROPE_FREQS = 16
S_HEADS = 8
S_HEAD_DIM = 64
S_WIDTH = 512
S_GROUPS = 2
S_STATE = 128
CHUNK = 64
D_FF = 2816
N_MOD = 6
NORM_EPS = 1e-6
LOG2E = 1.4426950408889634
LANES = 128
U_MAIN = 8192
GATE_W = 2 * LANES
DT_LANE0 = 64
CT_ROWS = M_HEAD_DIM + 16
_MG0, _MG1 = 2048, 2064
_DT0, _DT1 = 5136, 5152
VMEM_LIMIT = 56 * 1024 * 1024


def _cparams(sem):
    return pltpu.CompilerParams(dimension_semantics=sem, vmem_limit_bytes=VMEM_LIMIT)


def _sigmoid(x):
    return 1.0 / (1.0 + jnp.exp(-x))


def _silu(x):
    return x * _sigmoid(x)


def _softplus(x):
    return jnp.maximum(x, 0.0) + jnp.log1p(jnp.exp(-jnp.abs(x)))


def _dot(a, b):
    return jnp.dot(a.astype(BF16), b.astype(BF16), preferred_element_type=F32)


def _dot_nt(a, b):
    return lax.dot_general(a.astype(BF16), b.astype(BF16), (((1,), (1,)), ((), ())),
                           preferred_element_type=F32)


def _dwconv3(x, w, b=None, period=None):
    t = x.shape[0]
    row = lax.broadcasted_iota(jnp.int32, x.shape, 0)
    if period is not None and period < t:
        row = row % period
        last = period - 1
    else:
        last = t - 1
    prev = jnp.where(row == 0, 0.0, pltpu.roll(x, 1, 0))
    nxt = jnp.where(row == last, 0.0, pltpu.roll(x, t - 1, 0))
    y = prev * w[0:1] + x * w[1:2] + nxt * w[2:3]
    if b is not None:
        y = y + b
    return y


def _cumsum_rows(x, reverse):
    n = x.shape[0]
    row = lax.broadcasted_iota(jnp.int32, x.shape, 0)
    k = 1
    while k < n:
        if reverse:
            x = x + jnp.where(row < n - k, pltpu.roll(x, n - k, 0), 0.0)
        else:
            x = x + jnp.where(row >= k, pltpu.roll(x, k, 0), 0.0)
        k *= 2
    return x


def _rms(x, g):
    return x * lax.rsqrt(jnp.mean(x * x, axis=-1, keepdims=True) + NORM_EPS) * g


def _mod_kernel(c_ref, w_ref, b_ref, o_ref):
    o_ref[0] = _dot(_silu(c_ref[...]), w_ref[0]) + b_ref[0]


def _modulation(c_all, w_mod, b_mod):
    rows = c_all.shape[0]
    tn = 1536
    return pl.pallas_call(
        _mod_kernel,
        grid=(DEPTH, N_MOD * D_MODEL // tn),
        in_specs=[pl.BlockSpec((rows, D_MODEL), lambda l, j: (0, 0)),
                  pl.BlockSpec((1, D_MODEL, tn), lambda l, j: (l, 0, j)),
                  pl.BlockSpec((1, 1, tn), lambda l, j: (l, 0, j))],
        out_specs=pl.BlockSpec((1, rows, tn), lambda l, j: (l, 0, j)),
        out_shape=jax.ShapeDtypeStruct((DEPTH, rows, N_MOD * D_MODEL), F32),
        compiler_params=_cparams(("arbitrary", "arbitrary")),
        name="modulation",
    )(c_all, w_mod.astype(BF16), b_mod[:, None, :])


def _inproj_kernel(x_ref, sh_ref, sc_ref, g_ref, w_ref, ws_ref, u_ref, ug_ref, h_scr):
    @pl.when(pl.program_id(2) == 0)
    def _():
        h = _rms(x_ref[0], g_ref[...]) * (1.0 + sc_ref[0]) + sh_ref[0]
        hb = h.astype(BF16)
        h_scr[...] = hb
        ug_ref[0] = jnp.dot(hb, ws_ref[...], preferred_element_type=F32)

    u_ref[0] = jnp.dot(h_scr[...], w_ref[...], preferred_element_type=F32).astype(u_ref.dtype)


def _inproj(x, shift, scale, g, w_main, w_small):
    b, t, _ = x.shape
    tm = min(t, 1024)
    tn = 2048
    per_batch = shift.shape[0] > 1
    mod_map = (lambda bi, i, j: (bi, 0, 0)) if per_batch else (lambda bi, i, j: (0, 0, 0))
    return pl.pallas_call(
        _inproj_kernel,
        grid=(b, t // tm, U_MAIN // tn),
        in_specs=[pl.BlockSpec((1, tm, D_MODEL), lambda bi, i, j: (bi, i, 0)),
                  pl.BlockSpec((1, 1, D_MODEL), mod_map),
                  pl.BlockSpec((1, 1, D_MODEL), mod_map),
                  pl.BlockSpec((1, D_MODEL), lambda bi, i, j: (0, 0)),
                  pl.BlockSpec((D_MODEL, tn), lambda bi, i, j: (0, j)),
                  pl.BlockSpec((D_MODEL, GATE_W), lambda bi, i, j: (0, 0))],
        out_specs=[pl.BlockSpec((1, tm, tn), lambda bi, i, j: (bi, i, j)),
                   pl.BlockSpec((1, tm, GATE_W), lambda bi, i, j: (bi, i, 0))],
        out_shape=[jax.ShapeDtypeStruct((b, t, U_MAIN), BF16),
                   jax.ShapeDtypeStruct((b, t, GATE_W), F32)],
        scratch_shapes=[pltpu.VMEM((tm, D_MODEL), BF16)],
        compiler_params=_cparams(("parallel", "parallel", "arbitrary")),
        name="inproj",
    )(x, shift, scale, g, w_main, w_small)


def _cummax_rows(x, reverse):
    n = x.shape[0]
    row = lax.broadcasted_iota(jnp.int32, x.shape, 0)
    k = 1
    while k < n:
        if reverse:
            x = jnp.maximum(x, jnp.where(row < n - k, pltpu.roll(x, n - k, 0), -jnp.inf))
        else:
            x = jnp.maximum(x, jnp.where(row >= k, pltpu.roll(x, k, 0), -jnp.inf))
        k *= 2
    return x


def _mlstm_gates(d, gi, gf, m_old):
    tl = CHUNK - 1 if d == 0 else 0
    rev = d == 1
    b = _cumsum_rows(-_softplus(-gf), rev)
    ib = gi - b
    m_t = b + jnp.maximum(m_old, _cummax_rows(ib, rev))
    m_new = m_t[tl:tl + 1, :]
    b_last = b[tl:tl + 1, :]
    grp = lax.broadcasted_iota(jnp.int32, gi.shape, 1) // 8
    p = jnp.where(grp == 0, b - m_t,
                  jnp.where(grp == 1, jnp.exp(b + m_old - m_t),
                            jnp.where(grp == 2, jnp.exp(-m_t), jnp.exp(b_last - m_new + ib))))
    pt = jnp.concatenate([p, pltpu.roll(p, LANES - 1, 1)], axis=0).T
    decay = jnp.exp(b_last + m_old - m_new)
    used = lax.broadcasted_iota(jnp.int32, m_new.shape, 1) < 32
    return pt, ib, decay, jnp.where(used, m_new, 0.0)


def _mlstm_pair(d, p, emit, kp, qt, vt, pt, ib, decay, ct_s):
    def row(q):
        r = q * 8 + d * 4 + 2 * p
        return pt[r:r + 1, :]

    la, lb = d * 4 + 2 * p, d * 4 + 2 * p + 1
    lane2 = lax.broadcasted_iota(jnp.int32, (1, 2 * LANES), 1)
    decay_l = jnp.where(lane2 < LANES, decay[:, la:la + 1], decay[:, lb:lb + 1])
    klane = lax.broadcasted_iota(jnp.int32, kp.shape, 1)
    zero = jnp.zeros_like(kp)
    kbd = jnp.concatenate([jnp.where(klane < LANES, kp, zero), jnp.where(klane < LANES, zero, kp)], axis=0)
    ct_old = ct_s[d, p]
    ct_s[d, p] = decay_l * ct_old + jnp.dot((vt.astype(F32) * row(3)).astype(BF16), kbd,
                                            preferred_element_type=F32)
    if not emit:
        return None
    res = jnp.dot(jnp.concatenate([kp, ct_old.astype(BF16)], axis=0), qt, preferred_element_type=F32)
    st, cq = res[0:CHUNK], res[CHUNK:]
    r = lax.broadcasted_iota(jnp.int32, (CHUNK, LANES), 0)
    lane = lax.broadcasted_iota(jnp.int32, (CHUNK, LANES), 1)
    tt = lane % CHUNK
    mask = (r <= tt) if d == 0 else (r >= tt)
    ibp = jnp.where(lane < CHUNK, ib[:, la:la + 1], ib[:, lb:lb + 1])
    wt = jnp.where(mask, jnp.exp(row(0) + ibp), 0.0) * st
    wbd = jnp.concatenate([jnp.where(lane < CHUNK, wt, 0.0), jnp.where(lane < CHUNK, 0.0, wt)], axis=0)
    num = row(1) * cq + jnp.dot(vt, wbd.astype(BF16), preferred_element_type=F32)
    den = num[M_HEAD_DIM:M_HEAD_DIM + 1, :]
    return num[0:M_HEAD_DIM] / jnp.maximum(jnp.abs(den), row(2))


def _mlstm_kernel(emit_ctx,
                  q_ref, k_ref, v_ref, o_ref, g_ref, qc_ref, kc_ref, vc_ref, oc_ref, gc_ref,
                  cwq_ref, cwk_ref, cbq_ref, cbk_ref, gb_ref, ng_ref,
                  y_ref, yc_ref,
                  q_s, k_s, qc_s, kc_s, qt_s, vt_s, acct_s, ct_s, m_s):
    t_lat, t_ctx = q_ref.shape[1], qc_ref.shape[1]
    nc_ctx, nc_lat = t_ctx // CHUNK, t_lat // CHUNK
    kscale = M_HEAD_DIM ** -0.5

    for h in range(M_HEADS):
        sl = slice(h * LANES, (h + 1) * LANES)
        for src, dst, cw, cb, sc in ((q_ref, q_s, cwq_ref, cbq_ref, 1.0), (k_ref, k_s, cwk_ref, cbk_ref, kscale),
                                     (qc_ref, qc_s, cwq_ref, cbq_ref, 1.0), (kc_ref, kc_s, cwk_ref, cbk_ref, kscale)):
            y = _silu(_dwconv3(src[0, :, sl].astype(F32), cw[:, sl], cb[:, sl]))
            dst[:, sl] = (y * sc).astype(BF16) if sc != 1.0 else y.astype(BF16)

    def transposes(nchunks, base, qs, vr):
        def body(c, carry):
            rows = pl.ds(pl.multiple_of(c * CHUNK, CHUNK), CHUNK)
            for p in range(M_HEADS // 2):
                sl2 = slice(p * 2 * LANES, (p + 1) * 2 * LANES)
                qp = qs[rows, sl2].astype(F32)
                lane = lax.broadcasted_iota(jnp.int32, qp.shape, 1)
                qbd = jnp.concatenate([jnp.where(lane < LANES, qp, 0.0), jnp.where(lane < LANES, 0.0, qp)], axis=0)
                qt_s[base + c, p] = qbd.T.astype(BF16)
                vp = vr[0, rows, sl2].astype(F32)
                vt_s[base + c, p, 0:M_HEAD_DIM, :] = jnp.concatenate([vp[:, 0:LANES], vp[:, LANES:]],
                                                                     axis=0).T.astype(BF16)
                vt_s[base + c, p, M_HEAD_DIM:, :] = jnp.ones((CT_ROWS - M_HEAD_DIM, LANES), BF16)
            return carry
        lax.fori_loop(0, nchunks, body, 0, unroll=8 if nchunks >= 16 else 2)

    transposes(nc_ctx, 0, qc_s, vc_ref)
    transposes(nc_lat, nc_ctx, q_s, v_ref)
    ct_s[...] = jnp.zeros_like(ct_s)
    m_s[...] = jnp.zeros_like(m_s)

    def finish_pair(total_t, rows, p, oref, yref):
        ht = total_t.T
        for hh in range(2):
            sl = slice((2 * p + hh) * LANES, (2 * p + hh + 1) * LANES)
            hn = _rms(ht[hh * CHUNK:(hh + 1) * CHUNK], ng_ref[:, sl])
            yref[0, rows, sl] = (_sigmoid(oref[0, rows, sl].astype(F32)) * hn).astype(yref.dtype)

    def run(nchunks, base, emit, ks, gr, oref, yref):
        def make_body(final):
            def body(j, carry):
                for d in range(2):
                    c = j if d == 0 else nchunks - 1 - j
                    rows = pl.ds(pl.multiple_of(c * CHUNK, CHUNK), CHUNK)
                    pt, ib, decay, m_new = _mlstm_gates(d, gr[0, rows, 0:LANES] + gb_ref[:, 0:LANES],
                                                        gr[0, rows, LANES:] + gb_ref[:, LANES:], m_s[d])
                    m_s[d] = m_new
                    for p in range(M_HEADS // 2):
                        out = _mlstm_pair(d, p, emit, ks[rows, p * 2 * LANES:(p + 1) * 2 * LANES],
                                          qt_s[base + c, p], vt_s[base + c, p], pt, ib, decay, ct_s)
                        if emit and final:
                            finish_pair(acct_s[base + c, p] + out, rows, p, oref, yref)
                        elif emit:
                            acct_s[base + c, p] = out
                return carry
            return body

        half = nchunks // 2
        unroll = 8 if half >= 16 else 2
        lax.fori_loop(0, half, make_body(False), 0, unroll=unroll)
        lax.fori_loop(half, nchunks, make_body(True), 0, unroll=unroll)

    run(nc_ctx, 0, emit_ctx, kc_s, gc_ref, oc_ref, yc_ref)
    run(nc_lat, nc_ctx, True, k_s, g_ref, o_ref, y_ref)
    if not emit_ctx:
        yc_ref[...] = jnp.zeros_like(yc_ref)


def _mlstm(u, ug, uc, ugc, conv_w, conv_b, gate_bias, norm_g, emit_ctx):
    b, t, _ = u.shape
    tc = uc.shape[1]
    w = M_WIDTH
    nchunks = (t + tc) // CHUNK
    pairs = M_HEADS // 2

    def ublk(tt, col):
        return pl.BlockSpec((1, tt, w), lambda bi, col=col: (bi, 0, col))

    def full(shape):
        return pl.BlockSpec(shape, lambda bi: (0,) * len(shape))

    return pl.pallas_call(
        functools.partial(_mlstm_kernel, emit_ctx),
        grid=(b,),
        in_specs=[ublk(t, 0), ublk(t, 1), ublk(t, 2), ublk(t, 3),
                  pl.BlockSpec((1, t, 2 * LANES), lambda bi: (bi, 0, 0)),
                  ublk(tc, 0), ublk(tc, 1), ublk(tc, 2), ublk(tc, 3),
                  pl.BlockSpec((1, tc, 2 * LANES), lambda bi: (bi, 0, 0)),
                  pl.BlockSpec((3, w), lambda bi: (0, 0)), pl.BlockSpec((3, w), lambda bi: (0, 1)),
                  pl.BlockSpec((1, w), lambda bi: (0, 0)), pl.BlockSpec((1, w), lambda bi: (0, 1)),
                  full((1, 2 * LANES)), full((1, w))],
        out_specs=[pl.BlockSpec((1, t, w), lambda bi: (bi, 0, 0)),
                   pl.BlockSpec((1, tc, w), lambda bi: (bi, 0, 0))],
        out_shape=[jax.ShapeDtypeStruct((b, t, w), BF16), jax.ShapeDtypeStruct((b, tc, w), BF16)],
        scratch_shapes=[pltpu.VMEM((t, w), BF16), pltpu.VMEM((t, w), BF16),
                        pltpu.VMEM((tc, w), BF16), pltpu.VMEM((tc, w), BF16),
                        pltpu.VMEM((nchunks, pairs, 2 * M_HEAD_DIM, LANES), BF16),
                        pltpu.VMEM((nchunks, pairs, CT_ROWS, LANES), BF16),
                        pltpu.VMEM((nchunks, pairs, M_HEAD_DIM, LANES), F32),
                        pltpu.VMEM((2, pairs, CT_ROWS, 2 * LANES), F32),
                        pltpu.VMEM((2, 1, LANES), F32)],
        compiler_params=_cparams(("parallel",)),
        name="mlstm",
    )(u, u, u, u, ug, uc, uc, uc, uc, ugc, conv_w, conv_w, conv_b, conv_b, gate_bias, norm_g)


def _head_lanes(cols):
    rows = cols[0].shape[0]
    lane = lax.broadcasted_iota(jnp.int32, (rows, 4 * S_HEAD_DIM), 1) // S_HEAD_DIM
    out = jnp.broadcast_to(cols[3], lane.shape)
    for hh in (2, 1, 0):
        out = jnp.where(lane == hh, cols[hh], out)
    return out


def _ssd_stream(d, g, emit, xg, bg, cg, dtt, st, stt, st_s):
    tl = CHUNK - 1 if d == 0 else 0
    c0 = DT_LANE0 + d * S_HEADS + g * 4
    gw = 4 * S_HEAD_DIM
    scols = [st[:, c0 + hh:c0 + hh + 1] for hh in range(4)]
    s_l = _head_lanes(scols)
    dt_l = _head_lanes([dtt[:, c0 + hh:c0 + hh + 1] for hh in range(4)])
    s_last = s_l[tl:tl + 1, :]
    xdt = xg * dt_l
    bt4 = jnp.concatenate([bg] * 4, axis=0).T
    bt = bt4[:, 0:CHUNK]
    st_old = st_s[d, g]
    st_s[d, g] = jnp.exp(s_last) * st_old + _dot(bt, xdt * jnp.exp(s_last - s_l))
    if not emit:
        return None
    srow = jnp.concatenate([stt[c0 + hh:c0 + hh + 1, :] for hh in range(4)], axis=1)
    r = lax.broadcasted_iota(jnp.int32, (CHUNK, gw), 0)
    lane = lax.broadcasted_iota(jnp.int32, (CHUNK, gw), 1)
    sp = lane % S_HEAD_DIM
    mask = (sp <= r) if d == 0 else (sp >= r)
    seg = jnp.where(mask, jnp.exp(s_l - srow), 0.0)
    m_all = _dot(cg, bt4) * seg
    rr = lax.broadcasted_iota(jnp.int32, (gw, gw), 0) // S_HEAD_DIM
    ll = lax.broadcasted_iota(jnp.int32, (gw, gw), 1) // S_HEAD_DIM
    xd = jnp.where(rr == ll, jnp.concatenate([xdt] * 4, axis=0), 0.0)
    return _dot(m_all, xd) + jnp.exp(s_l) * _dot(cg, st_old)


def _ssd_kernel(emit_ctx,
                x_ref, bc_ref, z_ref, g_ref, xc_ref, bcc_ref, zc_ref, gc_ref,
                cwx_ref, cwb_ref, cbx_ref, cbb_ref, gb_ref, al_ref, dsk_ref, ng_ref,
                y_ref, yc_ref,
                x_s, bc_s, xc_s, bcc_s, acc_s, accc_s, st_s):
    t_lat, t_ctx = x_ref.shape[1], xc_ref.shape[1]
    gw = 4 * S_HEAD_DIM

    for c in range(S_WIDTH // LANES):
        sl = slice(c * LANES, (c + 1) * LANES)
        for src, dst, cw, cb in ((x_ref, x_s, cwx_ref, cbx_ref), (bc_ref, bc_s, cwb_ref, cbb_ref),
                                 (xc_ref, xc_s, cwx_ref, cbx_ref), (bcc_ref, bcc_s, cwb_ref, cbb_ref)):
            dst[:, sl] = _silu(_dwconv3(src[0, :, sl].astype(F32), cw[:, sl], cb[:, sl])).astype(dst.dtype)

    acc_s[...] = jnp.zeros_like(acc_s)
    accc_s[...] = jnp.zeros_like(accc_s)
    st_s[...] = jnp.zeros_like(st_s)
    acoef = -jnp.exp(al_ref[...])

    def run(nchunks, emit, xs, bcs, gr, acc):
        def body(j, carry):
            for d in range(2):
                c = j if d == 0 else nchunks - 1 - j
                rows = pl.ds(pl.multiple_of(c * CHUNK, CHUNK), CHUNK)
                dtt = _softplus(gr[0, rows, :] + gb_ref[...])
                st = _cumsum_rows(dtt * acoef, reverse=(d == 1))
                stt = st.T
                for g in range(S_GROUPS):
                    out = _ssd_stream(d, g, emit, xs[rows, g * gw:(g + 1) * gw],
                                      bcs[rows, g * S_STATE:(g + 1) * S_STATE],
                                      bcs[rows, (S_GROUPS + g) * S_STATE:(S_GROUPS + g + 1) * S_STATE],
                                      dtt, st, stt, st_s)
                    if emit:
                        acc[rows, g * gw:(g + 1) * gw] += out
            return carry
        lax.fori_loop(0, nchunks, body, 0, unroll=8 if nchunks >= 16 else 2)

    run(t_ctx // CHUNK, emit_ctx, xc_s, bcc_s, gc_ref, accc_s)
    run(t_lat // CHUNK, True, x_s, bc_s, g_ref, acc_s)

    def finish(acc, xs, zref, yref):
        for g in range(S_GROUPS):
            sl = slice(g * gw, (g + 1) * gw)
            yv = (acc[:, sl] + dsk_ref[:, sl] * xs[:, sl]) * _silu(zref[0, :, sl].astype(F32))
            yref[0, :, sl] = _rms(yv, ng_ref[:, sl]).astype(yref.dtype)

    finish(acc_s, x_s, z_ref, y_ref)
    if emit_ctx:
        finish(accc_s, xc_s, zc_ref, yc_ref)
    else:
        yc_ref[...] = jnp.zeros_like(yc_ref)


def _ssd(u, ug, uc, ugc, conv_w, conv_b, gate_bias, alog_row, dskip_row, norm_g, emit_ctx):
    b, t, _ = u.shape
    tc = uc.shape[1]
    w = S_WIDTH

    def ublk(tt, col):
        return pl.BlockSpec((1, tt, w), lambda bi, col=col: (bi, 0, col))

    def full(shape):
        return pl.BlockSpec(shape, lambda bi: (0,) * len(shape))

    return pl.pallas_call(
        functools.partial(_ssd_kernel, emit_ctx),
        grid=(b,),
        in_specs=[ublk(t, 8), ublk(t, 9), ublk(t, 7), pl.BlockSpec((1, t, LANES), lambda bi: (bi, 0, 0)),
                  ublk(tc, 8), ublk(tc, 9), ublk(tc, 7), pl.BlockSpec((1, tc, LANES), lambda bi: (bi, 0, 0)),
                  pl.BlockSpec((3, w), lambda bi: (0, 0)), pl.BlockSpec((3, w), lambda bi: (0, 1)),
                  pl.BlockSpec((1, w), lambda bi: (0, 0)), pl.BlockSpec((1, w), lambda bi: (0, 1)),
                  full((1, LANES)), full((1, LANES)), full((1, w)), full((1, w))],
        out_specs=[pl.BlockSpec((1, t, w), lambda bi: (bi, 0, 0)),
                   pl.BlockSpec((1, tc, w), lambda bi: (bi, 0, 0))],
        out_shape=[jax.ShapeDtypeStruct((b, t, w), BF16), jax.ShapeDtypeStruct((b, tc, w), BF16)],
        scratch_shapes=[pltpu.VMEM((t, w), F32), pltpu.VMEM((t, w), F32),
                        pltpu.VMEM((tc, w), F32), pltpu.VMEM((tc, w), F32),
                        pltpu.VMEM((t, w), F32), pltpu.VMEM((tc, w), F32),
                        pltpu.VMEM((2, S_GROUPS, S_STATE, 4 * S_HEAD_DIM), F32)],
        compiler_params=_cparams(("parallel",)),
        name="ssd",
    )(u, u, u, ug, uc, uc, uc, ugc, conv_w, conv_w, conv_b, conv_b, gate_bias, alog_row, dskip_row, norm_g)


def _qk_prep(x, g, cos, sin):
    lane = lax.broadcasted_iota(jnp.int32, x.shape, 1)
    lo = lane < A_SUB_DIM
    x2 = x * x
    s_lo = jnp.sum(jnp.where(lo, x2, 0.0), axis=-1, keepdims=True)
    s_hi = jnp.sum(jnp.where(lo, 0.0, x2), axis=-1, keepdims=True)
    r_lo = lax.rsqrt(s_lo * (1.0 / A_SUB_DIM) + NORM_EPS)
    r_hi = lax.rsqrt(s_hi * (1.0 / A_SUB_DIM) + NORM_EPS)
    y = x * jnp.where(lo, r_lo, r_hi) * g
    if cos is not None:
        swapped = jnp.where(lane % (2 * ROPE_FREQS) < ROPE_FREQS,
                            pltpu.roll(y, LANES - ROPE_FREQS, 1), pltpu.roll(y, ROPE_FREQS, 1))
        y = y * cos + swapped * sin
    return y


def _attn_kernel(lam_init, rope_q, src_rope, tq, *refs):
    n_src = len(src_rope)
    q_ref = refs[0]
    kv_refs = refs[1:1 + 2 * n_src]
    cos_ref, sin_ref, qg_ref, kg_ref, lam_ref, sg_ref = refs[1 + 2 * n_src:7 + 2 * n_src]
    o_ref = refs[7 + 2 * n_src]
    k_s, v_s = refs[8 + 2 * n_src:]
    qi = pl.program_id(1)

    @pl.when(qi == 0)
    def _():
        off = 0
        for si in range(n_src):
            k_ref, v_ref = kv_refs[2 * si], kv_refs[2 * si + 1]
            ts = k_ref.shape[1]
            for h in range(A_HEADS):
                sl = slice(h * LANES, (h + 1) * LANES)
                cs = (cos_ref[0:ts, :], sin_ref[0:ts, :]) if src_rope[si] else (None, None)
                k_s[h, off:off + ts, :] = _qk_prep(k_ref[0, :, sl].astype(F32), kg_ref[...], *cs).astype(BF16)
                v_s[h, off:off + ts, 0:LANES] = v_ref[0, :, sl]
                v_s[h, off:off + ts, LANES:2 * LANES] = jnp.ones((ts, LANES), BF16)
            off += ts

    lp = lam_ref[...]
    lam = (jnp.exp(jnp.sum(lp[0:1] * lp[1:2], axis=-1, keepdims=True))
           - jnp.exp(jnp.sum(lp[2:3] * lp[3:4], axis=-1, keepdims=True)) + lam_init)
    if rope_q:
        rows = pl.ds(pl.multiple_of(qi * tq, tq), tq)
        cs = (cos_ref[rows, :], sin_ref[rows, :])
    else:
        cs = (None, None)
    for h in range(A_HEADS):
        sl = slice(h * LANES, (h + 1) * LANES)
        qh = _qk_prep(q_ref[0, :, sl].astype(F32), qg_ref[...], *cs) * (A_SUB_DIM ** -0.5 * LOG2E)
        lo = lax.broadcasted_iota(jnp.int32, qh.shape, 1) < A_SUB_DIM
        q2 = jnp.concatenate([jnp.where(lo, qh, 0.0), jnp.where(lo, 0.0, qh)], axis=0).astype(BF16)
        s = _dot_nt(q2, k_s[h])
        e = jnp.exp2(s - jnp.max(s, axis=-1, keepdims=True)).astype(BF16)
        pv = jnp.dot(e, v_s[h], preferred_element_type=F32)
        o = (pv[0:tq, 0:LANES] / pv[0:tq, LANES:2 * LANES]
             - lam * (pv[tq:2 * tq, 0:LANES] / pv[tq:2 * tq, LANES:2 * LANES]))
        o_ref[0, :, sl] = (_rms(o, sg_ref[...]) * (1.0 - lam_init)).astype(o_ref.dtype)


def _attention(uq, sources, cos, sin, qg, kg, lam_p, sub_g, lam_init, rope_q):
    b, t, _ = uq.shape
    tq = min(t, 256)
    w = A_WIDTH
    tk = sum(s[0].shape[1] for s in sources)
    in_specs = [pl.BlockSpec((1, tq, w), lambda bi, qi: (bi, qi, 4))]
    args = [uq]
    for arr, _ in sources:
        ts = arr.shape[1]
        in_specs += [pl.BlockSpec((1, ts, w), lambda bi, qi: (bi, 0, 5)),
                     pl.BlockSpec((1, ts, w), lambda bi, qi: (bi, 0, 6))]
        args += [arr, arr]
    tcs = cos.shape[0]
    for shape in ((tcs, LANES), (tcs, LANES), (1, LANES), (1, LANES), (4, A_SUB_DIM), (1, LANES)):
        in_specs.append(pl.BlockSpec(shape, lambda bi, qi: (0, 0)))
    args += [cos, sin, qg, kg, lam_p, sub_g]
    return pl.pallas_call(
        functools.partial(_attn_kernel, lam_init, rope_q, tuple(r for _, r in sources), tq),
        grid=(b, t // tq),
        in_specs=in_specs,
        out_specs=pl.BlockSpec((1, tq, w), lambda bi, qi: (bi, qi, 0)),
        out_shape=jax.ShapeDtypeStruct((b, t, w), BF16),
        scratch_shapes=[pltpu.VMEM((A_HEADS, tk, LANES), BF16), pltpu.VMEM((A_HEADS, tk, 2 * LANES), BF16)],
        compiler_params=_cparams(("parallel", "arbitrary")),
        name="diff_attention",
    )(*args)


def _merge_kernel(x_ref, ym_ref, ya_ref, ys_ref, gm_ref, ga_ref, gs_ref, m2_ref,
                  wm_ref, wa_ref, ws_ref, wo_ref, o_ref):
    t = (_sigmoid(gm_ref[0].astype(F32)) * _dot(ym_ref[0], wm_ref[...])
         + _sigmoid(ga_ref[0].astype(F32)) * _dot(ya_ref[0], wa_ref[...])
         + _sigmoid(gs_ref[0].astype(F32)) * _dot(ys_ref[0], ws_ref[...]))
    o_ref[0] = x_ref[0] + m2_ref[0] * _dot(t, wo_ref[...])


def _merge(x, y_m, y_a, y_s, u, mod2, w_m, w_a, w_s, w_o):
    b, t, _ = x.shape
    tm = min(t, 512)
    per_batch = mod2.shape[0] > 1
    mod_map = (lambda bi, i: (bi, 0, 0)) if per_batch else (lambda bi, i: (0, 0, 0))

    def yblk():
        return pl.BlockSpec((1, tm, 512), lambda bi, i: (bi, i, 0))

    def gblk(col):
        return pl.BlockSpec((1, tm, D_MODEL), lambda bi, i, col=col: (bi, i, col))

    def wblk(k):
        return pl.BlockSpec((k, D_MODEL), lambda bi, i: (0, 0))

    return pl.pallas_call(
        _merge_kernel,
        grid=(b, t // tm),
        in_specs=[pl.BlockSpec((1, tm, D_MODEL), lambda bi, i: (bi, i, 0)), yblk(), yblk(), yblk(),
                  gblk(5), gblk(6), gblk(7), pl.BlockSpec((1, 1, D_MODEL), mod_map),
                  wblk(512), wblk(512), wblk(512), wblk(D_MODEL)],
        out_specs=pl.BlockSpec((1, tm, D_MODEL), lambda bi, i: (bi, i, 0)),
        out_shape=jax.ShapeDtypeStruct(x.shape, F32),
        compiler_params=_cparams(("parallel", "parallel")),
        name="merge",
    )(x, y_m, y_a, y_s, u, u, u, mod2, w_m, w_a, w_s, w_o)


def _ffn_kernel(seq, x_ref, sh_ref, sc_ref, gt_ref, g_ref, wa_ref, wg_ref, cwa_ref, cwg_ref, wd_ref, o_ref, h_s):
    j = pl.program_id(1)
    last = pl.num_programs(1) - 1

    @pl.when(j == 0)
    def _():
        h = _rms(x_ref[0], g_ref[...]) * (1.0 + sc_ref[0]) + sh_ref[0]
        h_s[...] = h.astype(BF16)
        o_ref[...] = jnp.zeros_like(o_ref)

    a = _dwconv3(jnp.dot(h_s[...], wa_ref[...], preferred_element_type=F32), cwa_ref[...], period=seq)
    g = _dwconv3(jnp.dot(h_s[...], wg_ref[...], preferred_element_type=F32), cwg_ref[...], period=seq)
    o_ref[0] += _dot(_silu(g) * a, wd_ref[...])

    @pl.when(j == last)
    def _():
        o_ref[0] = x_ref[0] + gt_ref[0] * o_ref[0]


def _ffn(x, shift, scale, gate, g, w_up, conv_w, w_down, seq=None):
    b, t, _ = x.shape
    tf = 256
    nf = D_FF // tf
    per_batch = shift.shape[0] > 1
    mod_map = (lambda bi, j: (bi, 0, 0)) if per_batch else (lambda bi, j: (0, 0, 0))
    mod_spec = pl.BlockSpec((1, 1, D_MODEL), mod_map)
    return pl.pallas_call(
        functools.partial(_ffn_kernel, t if seq is None else seq),
        grid=(b, nf),
        in_specs=[pl.BlockSpec((1, t, D_MODEL), lambda bi, j: (bi, 0, 0)), mod_spec, mod_spec, mod_spec,
                  pl.BlockSpec((1, D_MODEL), lambda bi, j: (0, 0)),
                  pl.BlockSpec((D_MODEL, tf), lambda bi, j: (0, j)),
                  pl.BlockSpec((D_MODEL, tf), lambda bi, j: (0, nf + j)),
                  pl.BlockSpec((3, tf), lambda bi, j: (0, j)),
                  pl.BlockSpec((3, tf), lambda bi, j: (0, nf + j)),
                  pl.BlockSpec((tf, D_MODEL), lambda bi, j: (j, 0))],
        out_specs=pl.BlockSpec((1, t, D_MODEL), lambda bi, j: (bi, 0, 0)),
        out_shape=jax.ShapeDtypeStruct(x.shape, F32),
        scratch_shapes=[pltpu.VMEM((t, D_MODEL), BF16)],
        compiler_params=_cparams(("parallel", "arbitrary")),
        name="conv_ffn",
    )(x, shift, scale, gate, g, w_up, w_up, conv_w, conv_w, w_down)


def _rope_tables(t):
    pos = jnp.arange(t)
    inv_freq = ROPE_BASE ** (-jnp.arange(ROPE_FREQS, dtype=F32) / ROPE_FREQS)
    ang = jnp.stack([pos // GRID_W, pos % GRID_W], axis=-1).astype(F32)[..., None] * inv_freq
    cos, sin = jnp.cos(ang), jnp.sin(ang)
    cos64 = jnp.concatenate([cos[:, 0], cos[:, 0], cos[:, 1], cos[:, 1]], axis=-1)
    sin64 = jnp.concatenate([-sin[:, 0], sin[:, 0], -sin[:, 1], sin[:, 1]], axis=-1)
    return jnp.tile(cos64, (1, 2)), jnp.tile(sin64, (1, 2))


def _gate_lanes():
    idx = [-1] * GATE_W
    for q in range(4):
        for d in range(2):
            for h in range(M_HEADS):
                idx[q * 8 + d * 4 + h] = d * M_HEADS + h
                idx[LANES + q * 8 + d * 4 + h] = 2 * M_HEADS + d * M_HEADS + h
    for d in range(2):
        for h in range(S_HEADS):
            idx[DT_LANE0 + d * S_HEADS + h] = 4 * M_HEADS + d * S_HEADS + h
    return idx


def _gate_row(i_vals, f_vals, dt_vals):
    idx = jnp.array(_gate_lanes())
    src = jnp.concatenate([i_vals.reshape(-1), f_vals.reshape(-1), dt_vals.reshape(-1)])
    return jnp.where(idx >= 0, src[jnp.maximum(idx, 0)], 0.0)[None, :]


def _gate_weights(wl):
    idx = jnp.array(_gate_lanes())
    mg = wl[:, _MG0:_MG1].reshape(D_MODEL, 2, 2, M_HEADS)
    src = jnp.concatenate([mg[:, :, 0].reshape(D_MODEL, -1), mg[:, :, 1].reshape(D_MODEL, -1),
                           wl[:, _DT0:_DT1]], axis=1)
    return jnp.where(idx >= 0, src[:, jnp.maximum(idx, 0)], 0.0)


def kernel(x, c, ctx, c_ctx, w_mod, b_mod, norm1_g, norm2_g, w_in, m_conv_w, m_conv_b, m_igate_b, m_fgate_b, m_norm_g, a_qnorm_g, a_knorm_g, a_lambda, a_subln_g, s_conv_w, s_conv_b, s_dt_bias, s_a_log, s_d, s_norm_g, w_branch_m, w_branch_a, w_branch_s, w_out, w_up, ffn_conv_w, w_down):
    b, t, _ = x.shape
    cos, sin = _rope_tables(t)

    c_all = jnp.concatenate([c, c_ctx[None, :], jnp.zeros((7, D_MODEL), F32)], axis=0)
    mod_all = _modulation(c_all, w_mod, b_mod)

    xc = ctx
    for l in range(DEPTH):
        emit = l < DEPTH - 1
        lam_init = 0.8 - 0.6 * math.exp(-0.3 * l)
        mod = [mod_all[l, :b, i * D_MODEL:(i + 1) * D_MODEL][:, None, :] for i in range(N_MOD)]
        modc = [mod_all[l, b:b + 1, i * D_MODEL:(i + 1) * D_MODEL][:, None, :] for i in range(N_MOD)]

        wl = w_in[l]
        w_main = jnp.concatenate([wl[:, :_MG0], wl[:, _MG1:_DT0], wl[:, _DT1:]], axis=1).astype(BF16)
        w_small = _gate_weights(wl).astype(BF16)
        zeros8 = jnp.zeros((2 * M_HEADS,), F32)
        gate_bias = _gate_row(m_igate_b[l], m_fgate_b[l], s_dt_bias[l])
        alog_row = _gate_row(zeros8, zeros8, s_a_log[l])[:, :LANES]
        dskip_row = jnp.repeat(s_d[l], S_HEAD_DIM)[None, :]
        qg = jnp.tile(a_qnorm_g[l], 2)[None, :]
        kg = jnp.tile(a_knorm_g[l], 2)[None, :]

        n1 = norm1_g[l][None, :]
        u, ug = _inproj(x, mod[0], mod[1], n1, w_main, w_small)
        tc = xc.shape[1]
        uc, ugc = _inproj(xc.reshape(1, b * tc, D_MODEL), modc[0], modc[1], n1, w_main, w_small)
        uc, ugc = uc.reshape(b, tc, U_MAIN), ugc.reshape(b, tc, GATE_W)

        y_m, y_mc = _mlstm(u, ug, uc, ugc, m_conv_w[l], m_conv_b[l][None, :], gate_bias,
                           m_norm_g[l][None, :], emit)
        y_s, y_sc = _ssd(u, ug, uc, ugc, s_conv_w[l], s_conv_b[l][None, :], gate_bias, alog_row,
                         dskip_row, s_norm_g[l][None, :], emit)
        y_a = _attention(u, [(u, True), (uc, False)], cos, sin, qg, kg, a_lambda[l],
                         a_subln_g[l][None, :], lam_init, True)

        wm, wa, ws, wo = (w_branch_m[l].astype(BF16), w_branch_a[l].astype(BF16),
                          w_branch_s[l].astype(BF16), w_out[l].astype(BF16))
        wu, wd = w_up[l].astype(BF16), w_down[l].astype(BF16)
        n2 = norm2_g[l][None, :]
        x = _merge(x, y_m, y_a, y_s, u, mod[2], wm, wa, ws, wo)
        x = _ffn(x, mod[3], mod[4], mod[5], n2, wu, ffn_conv_w[l], wd)

        if emit:
            y_ac = _attention(uc, [(uc, False)], cos, sin, qg, kg, a_lambda[l],
                              a_subln_g[l][None, :], lam_init, False)
            flat = lambda a: a.reshape(1, b * tc, a.shape[-1])
            xc = _merge(flat(xc), flat(y_mc), flat(y_ac), flat(y_sc), flat(uc), modc[2], wm, wa, ws, wo)
            pack = math.gcd(b, max(1, 2048 // tc))
            xc = _ffn(xc.reshape(b // pack, pack * tc, D_MODEL), modc[3], modc[4], modc[5], n2, wu,
                      ffn_conv_w[l], wd, seq=tc).reshape(b, tc, D_MODEL)
    return x
```

```python
import functools
import math

import jax
import jax.numpy as jnp
from jax import lax
from jax.experimental import pallas as pl
from jax.experimental.pallas import tpu as pltpu

F32 = jnp.float32
BF16 = jnp.bfloat16

D_MODEL = 1024
DEPTH = 2
GRID_W = 64
M_HEADS = 4
M_HEAD_DIM = 128
M_WIDTH = 512
A_HEADS = 4
A_SUB_DIM = 64
A_V_DIM = 128
A_WIDTH = 512
ROPE_BASE = 10000.0
ROPE_FREQS = 16
S_HEADS = 8
S_HEAD_DIM = 64
S_WIDTH = 512
S_GROUPS = 2
S_STATE = 128
CHUNK = 64
D_FF = 2816
N_MOD = 6
NORM_EPS = 1e-6
LOG2E = 1.4426950408889634
LANES = 128
U_MAIN = 8192
GATE_W = 2 * LANES
DT_LANE0 = 64
CT_ROWS = M_HEAD_DIM + 16
_MG0, _MG1 = 2048, 2064
_DT0, _DT1 = 5136, 5152
VMEM_LIMIT = 56 * 1024 * 1024


def _cparams(sem):
    return pltpu.CompilerParams(dimension_semantics=sem, vmem_limit_bytes=VMEM_LIMIT)


def _sigmoid(x):
    return 1.0 / (1.0 + jnp.exp(-x))


def _silu(x):
    return x * _sigmoid(x)


def _softplus(x):
    return jnp.maximum(x, 0.0) + jnp.log1p(jnp.exp(-jnp.abs(x)))


def _dot(a, b):
    return jnp.dot(a.astype(BF16), b.astype(BF16), preferred_element_type=F32)


def _dot_nt(a, b):
    return lax.dot_general(a.astype(BF16), b.astype(BF16), (((1,), (1,)), ((), ())),
                           preferred_element_type=F32)


def _dwconv3(x, w, b=None, period=None):
    t = x.shape[0]
    row = lax.broadcasted_iota(jnp.int32, x.shape, 0)
    if period is not None and period < t:
        row = row % period
        last = period - 1
    else:
        last = t - 1
    prev = jnp.where(row == 0, 0.0, pltpu.roll(x, 1, 0))
    nxt = jnp.where(row == last, 0.0, pltpu.roll(x, t - 1, 0))
    y = prev * w[0:1] + x * w[1:2] + nxt * w[2:3]
    if b is not None:
        y = y + b
    return y


def _cumsum_rows(x, reverse):
    n = x.shape[0]
    row = lax.broadcasted_iota(jnp.int32, x.shape, 0)
    k = 1
    while k < n:
        if reverse:
            x = x + jnp.where(row < n - k, pltpu.roll(x, n - k, 0), 0.0)
        else:
            x = x + jnp.where(row >= k, pltpu.roll(x, k, 0), 0.0)
        k *= 2
    return x


def _rms(x, g):
    return x * lax.rsqrt(jnp.mean(x * x, axis=-1, keepdims=True) + NORM_EPS) * g


def _mod_kernel(c_ref, w_ref, b_ref, o_ref):
    o_ref[0] = _dot(_silu(c_ref[...]), w_ref[0]) + b_ref[0]


def _modulation(c_all, w_mod, b_mod):
    rows = c_all.shape[0]
    tn = 1536
    return pl.pallas_call(
        _mod_kernel,
        grid=(DEPTH, N_MOD * D_MODEL // tn),
        in_specs=[pl.BlockSpec((rows, D_MODEL), lambda l, j: (0, 0)),
                  pl.BlockSpec((1, D_MODEL, tn), lambda l, j: (l, 0, j)),
                  pl.BlockSpec((1, 1, tn), lambda l, j: (l, 0, j))],
        out_specs=pl.BlockSpec((1, rows, tn), lambda l, j: (l, 0, j)),
        out_shape=jax.ShapeDtypeStruct((DEPTH, rows, N_MOD * D_MODEL), F32),
        compiler_params=_cparams(("arbitrary", "arbitrary")),
        name="modulation",
    )(c_all, w_mod.astype(BF16), b_mod[:, None, :])


def _inproj_kernel(x_ref, sh_ref, sc_ref, g_ref, w_ref, ws_ref, u_ref, ug_ref, h_scr):
    @pl.when(pl.program_id(2) == 0)
    def _():
        h = _rms(x_ref[0], g_ref[...]) * (1.0 + sc_ref[0]) + sh_ref[0]
        hb = h.astype(BF16)
        h_scr[...] = hb
        ug_ref[0] = jnp.dot(hb, ws_ref[...], preferred_element_type=F32)

    u_ref[0] = jnp.dot(h_scr[...], w_ref[...], preferred_element_type=F32).astype(u_ref.dtype)


def _inproj(x, shift, scale, g, w_main, w_small):
    b, t, _ = x.shape
    tm = min(t, 1024)
    tn = 2048
    per_batch = shift.shape[0] > 1
    mod_map = (lambda bi, i, j: (bi, 0, 0)) if per_batch else (lambda bi, i, j: (0, 0, 0))
    return pl.pallas_call(
        _inproj_kernel,
        grid=(b, t // tm, U_MAIN // tn),
        in_specs=[pl.BlockSpec((1, tm, D_MODEL), lambda bi, i, j: (bi, i, 0)),
                  pl.BlockSpec((1, 1, D_MODEL), mod_map),
                  pl.BlockSpec((1, 1, D_MODEL), mod_map),
                  pl.BlockSpec((1, D_MODEL), lambda bi, i, j: (0, 0)),
                  pl.BlockSpec((D_MODEL, tn), lambda bi, i, j: (0, j)),
                  pl.BlockSpec((D_MODEL, GATE_W), lambda bi, i, j: (0, 0))],
        out_specs=[pl.BlockSpec((1, tm, tn), lambda bi, i, j: (bi, i, j)),
                   pl.BlockSpec((1, tm, GATE_W), lambda bi, i, j: (bi, i, 0))],
        out_shape=[jax.ShapeDtypeStruct((b, t, U_MAIN), BF16),
                   jax.ShapeDtypeStruct((b, t, GATE_W), F32)],
        scratch_shapes=[pltpu.VMEM((tm, D_MODEL), BF16)],
        compiler_params=_cparams(("parallel", "parallel", "arbitrary")),
        name="inproj",
    )(x, shift, scale, g, w_main, w_small)


def _cummax_rows(x, reverse):
    n = x.shape[0]
    row = lax.broadcasted_iota(jnp.int32, x.shape, 0)
    k = 1
    while k < n:
        if reverse:
            x = jnp.maximum(x, jnp.where(row < n - k, pltpu.roll(x, n - k, 0), -jnp.inf))
        else:
            x = jnp.maximum(x, jnp.where(row >= k, pltpu.roll(x, k, 0), -jnp.inf))
        k *= 2
    return x


def _mlstm_gates(d, gi, gf, m_old):
    tl = CHUNK - 1 if d == 0 else 0
    rev = d == 1
    b = _cumsum_rows(-_softplus(-gf), rev)
    ib = gi - b
    m_t = b + jnp.maximum(m_old, _cummax_rows(ib, rev))
    m_new = m_t[tl:tl + 1, :]
    b_last = b[tl:tl + 1, :]
    grp = lax.broadcasted_iota(jnp.int32, gi.shape, 1) // 8
    p = jnp.where(grp == 0, b - m_t,
                  jnp.where(grp == 1, jnp.exp(b + m_old - m_t),
                            jnp.where(grp == 2, jnp.exp(-m_t), jnp.exp(b_last - m_new + ib))))
    pt = jnp.concatenate([p, pltpu.roll(p, LANES - 1, 1)], axis=0).T
    decay = jnp.exp(b_last + m_old - m_new)
    used = lax.broadcasted_iota(jnp.int32, m_new.shape, 1) < 32
    return pt, ib, decay, jnp.where(used, m_new, 0.0)


def _mlstm_pair(d, p, emit, kp, qt, vt, pt, ib, decay, ct_s):
    def row(q):
        r = q * 8 + d * 4 + 2 * p
        return pt[r:r + 1, :]

    la, lb = d * 4 + 2 * p, d * 4 + 2 * p + 1
    lane2 = lax.broadcasted_iota(jnp.int32, (1, 2 * LANES), 1)
    decay_l = jnp.where(lane2 < LANES, decay[:, la:la + 1], decay[:, lb:lb + 1])
    klane = lax.broadcasted_iota(jnp.int32, kp.shape, 1)
    zero = jnp.zeros_like(kp)
    kbd = jnp.concatenate([jnp.where(klane < LANES, kp, zero), jnp.where(klane < LANES, zero, kp)], axis=0)
    ct_old = ct_s[d, p]
    ct_s[d, p] = decay_l * ct_old + jnp.dot((vt.astype(F32) * row(3)).astype(BF16), kbd,
                                            preferred_element_type=F32)
    if not emit:
        return None
    res = jnp.dot(jnp.concatenate([kp, ct_old.astype(BF16)], axis=0), qt, preferred_element_type=F32)
    st, cq = res[0:CHUNK], res[CHUNK:]
    r = lax.broadcasted_iota(jnp.int32, (CHUNK, LANES), 0)
    lane = lax.broadcasted_iota(jnp.int32, (CHUNK, LANES), 1)
    tt = lane % CHUNK
    mask = (r <= tt) if d == 0 else (r >= tt)
    ibp = jnp.where(lane < CHUNK, ib[:, la:la + 1], ib[:, lb:lb + 1])
    wt = jnp.where(mask, jnp.exp(row(0) + ibp), 0.0) * st
    wbd = jnp.concatenate([jnp.where(lane < CHUNK, wt, 0.0), jnp.where(lane < CHUNK, 0.0, wt)], axis=0)
    num = row(1) * cq + jnp.dot(vt, wbd.astype(BF16), preferred_element_type=F32)
    den = num[M_HEAD_DIM:M_HEAD_DIM + 1, :]
    return num[0:M_HEAD_DIM] / jnp.maximum(jnp.abs(den), row(2))


def _mlstm_kernel(emit_ctx,
                  q_ref, k_ref, v_ref, o_ref, g_ref, qc_ref, kc_ref, vc_ref, oc_ref, gc_ref,
                  cwq_ref, cwk_ref, cbq_ref, cbk_ref, gb_ref, ng_ref,
                  y_ref, yc_ref,
                  q_s, k_s, qc_s, kc_s, qt_s, vt_s, acct_s, ct_s, m_s):
    t_lat, t_ctx = q_ref.shape[1], qc_ref.shape[1]
    nc_ctx, nc_lat = t_ctx // CHUNK, t_lat // CHUNK
    kscale = M_HEAD_DIM ** -0.5

    for h in range(M_HEADS):
        sl = slice(h * LANES, (h + 1) * LANES)
        for src, dst, cw, cb, sc in ((q_ref, q_s, cwq_ref, cbq_ref, 1.0), (k_ref, k_s, cwk_ref, cbk_ref, kscale),
                                     (qc_ref, qc_s, cwq_ref, cbq_ref, 1.0), (kc_ref, kc_s, cwk_ref, cbk_ref, kscale)):
            y = _silu(_dwconv3(src[0, :, sl].astype(F32), cw[:, sl], cb[:, sl]))
            dst[:, sl] = (y * sc).astype(BF16) if sc != 1.0 else y.astype(BF16)

    def transposes(nchunks, base, qs, vr):
        def body(c, carry):
            rows = pl.ds(pl.multiple_of(c * CHUNK, CHUNK), CHUNK)
            for p in range(M_HEADS // 2):
                sl2 = slice(p * 2 * LANES, (p + 1) * 2 * LANES)
                qp = qs[rows, sl2].astype(F32)
                lane = lax.broadcasted_iota(jnp.int32, qp.shape, 1)
                qbd = jnp.concatenate([jnp.where(lane < LANES, qp, 0.0), jnp.where(lane < LANES, 0.0, qp)], axis=0)
                qt_s[base + c, p] = qbd.T.astype(BF16)
                vp = vr[0, rows, sl2].astype(F32)
                vt_s[base + c, p, 0:M_HEAD_DIM, :] = jnp.concatenate([vp[:, 0:LANES], vp[:, LANES:]],
                                                                     axis=0).T.astype(BF16)
                vt_s[base + c, p, M_HEAD_DIM:, :] = jnp.ones((CT_ROWS - M_HEAD_DIM, LANES), BF16)
            return carry
        lax.fori_loop(0, nchunks, body, 0, unroll=8 if nchunks >= 16 else 2)

    transposes(nc_ctx, 0, qc_s, vc_ref)
    transposes(nc_lat, nc_ctx, q_s, v_ref)
    ct_s[...] = jnp.zeros_like(ct_s)
    m_s[...] = jnp.zeros_like(m_s)

    def finish_pair(total_t, rows, p, oref, yref):
        ht = total_t.T
        for hh in range(2):
            sl = slice((2 * p + hh) * LANES, (2 * p + hh + 1) * LANES)
            hn = _rms(ht[hh * CHUNK:(hh + 1) * CHUNK], ng_ref[:, sl])
            yref[0, rows, sl] = (_sigmoid(oref[0, rows, sl].astype(F32)) * hn).astype(yref.dtype)

    def run(nchunks, base, emit, ks, gr, oref, yref):
        def make_body(final):
            def body(j, carry):
                for d in range(2):
                    c = j if d == 0 else nchunks - 1 - j
                    rows = pl.ds(pl.multiple_of(c * CHUNK, CHUNK), CHUNK)
                    pt, ib, decay, m_new = _mlstm_gates(d, gr[0, rows, 0:LANES] + gb_ref[:, 0:LANES],
                                                        gr[0, rows, LANES:] + gb_ref[:, LANES:], m_s[d])
                    m_s[d] = m_new
                    for p in range(M_HEADS // 2):
                        out = _mlstm_pair(d, p, emit, ks[rows, p * 2 * LANES:(p + 1) * 2 * LANES],
                                          qt_s[base + c, p], vt_s[base + c, p], pt, ib, decay, ct_s)
                        if emit and final:
                            finish_pair(acct_s[base + c, p] + out, rows, p, oref, yref)
                        elif emit:
                            acct_s[base + c, p] = out
                return carry
            return body

        half = nchunks // 2
        unroll = 8 if half >= 16 else 2
        lax.fori_loop(0, half, make_body(False), 0, unroll=unroll)
        lax.fori_loop(half, nchunks, make_body(True), 0, unroll=unroll)

    run(nc_ctx, 0, emit_ctx, kc_s, gc_ref, oc_ref, yc_ref)
    run(nc_lat, nc_ctx, True, k_s, g_ref, o_ref, y_ref)
    if not emit_ctx:
        yc_ref[...] = jnp.zeros_like(yc_ref)


def _mlstm(u, ug, uc, ugc, conv_w, conv_b, gate_bias, norm_g, emit_ctx):
    b, t, _ = u.shape
    tc = uc.shape[1]
    w = M_WIDTH
    nchunks = (t + tc) // CHUNK
    pairs = M_HEADS // 2

    def ublk(tt, col):
        return pl.BlockSpec((1, tt, w), lambda bi, col=col: (bi, 0, col))

    def full(shape):
        return pl.BlockSpec(shape, lambda bi: (0,) * len(shape))

    return pl.pallas_call(
        functools.partial(_mlstm_kernel, emit_ctx),
        grid=(b,),
        in_specs=[ublk(t, 0), ublk(t, 1), ublk(t, 2), ublk(t, 3),
                  pl.BlockSpec((1, t, 2 * LANES), lambda bi: (bi, 0, 0)),
                  ublk(tc, 0), ublk(tc, 1), ublk(tc, 2), ublk(tc, 3),
                  pl.BlockSpec((1, tc, 2 * LANES), lambda bi: (bi, 0, 0)),
                  pl.BlockSpec((3, w), lambda bi: (0, 0)), pl.BlockSpec((3, w), lambda bi: (0, 1)),
                  pl.BlockSpec((1, w), lambda bi: (0, 0)), pl.BlockSpec((1, w), lambda bi: (0, 1)),
                  full((1, 2 * LANES)), full((1, w))],
        out_specs=[pl.BlockSpec((1, t, w), lambda bi: (bi, 0, 0)),
                   pl.BlockSpec((1, tc, w), lambda bi: (bi, 0, 0))],
        out_shape=[jax.ShapeDtypeStruct((b, t, w), BF16), jax.ShapeDtypeStruct((b, tc, w), BF16)],
        scratch_shapes=[pltpu.VMEM((t, w), BF16), pltpu.VMEM((t, w), BF16),
                        pltpu.VMEM((tc, w), BF16), pltpu.VMEM((tc, w), BF16),
                        pltpu.VMEM((nchunks, pairs, 2 * M_HEAD_DIM, LANES), BF16),
                        pltpu.VMEM((nchunks, pairs, CT_ROWS, LANES), BF16),
                        pltpu.VMEM((nchunks, pairs, M_HEAD_DIM, LANES), F32),
                        pltpu.VMEM((2, pairs, CT_ROWS, 2 * LANES), F32),
                        pltpu.VMEM((2, 1, LANES), F32)],
        compiler_params=_cparams(("parallel",)),
        name="mlstm",
    )(u, u, u, u, ug, uc, uc, uc, uc, ugc, conv_w, conv_w, conv_b, conv_b, gate_bias, norm_g)


def _head_lanes(cols):
    rows = cols[0].shape[0]
    lane = lax.broadcasted_iota(jnp.int32, (rows, 4 * S_HEAD_DIM), 1) // S_HEAD_DIM
    out = jnp.broadcast_to(cols[3], lane.shape)
    for hh in (2, 1, 0):
        out = jnp.where(lane == hh, cols[hh], out)
    return out


def _ssd_stream(d, g, emit, xg, bg, cg, dtt, st, stt, st_s):
    tl = CHUNK - 1 if d == 0 else 0
    c0 = DT_LANE0 + d * S_HEADS + g * 4
    gw = 4 * S_HEAD_DIM
    scols = [st[:, c0 + hh:c0 + hh + 1] for hh in range(4)]
    s_l = _head_lanes(scols)
    dt_l = _head_lanes([dtt[:, c0 + hh:c0 + hh + 1] for hh in range(4)])
    s_last = s_l[tl:tl + 1, :]
    xdt = xg * dt_l
    bt4 = jnp.concatenate([bg] * 4, axis=0).T
    bt = bt4[:, 0:CHUNK]
    st_old = st_s[d, g]
    st_s[d, g] = jnp.exp(s_last) * st_old + _dot(bt, xdt * jnp.exp(s_last - s_l))
    if not emit:
        return None
    srow = jnp.concatenate([stt[c0 + hh:c0 + hh + 1, :] for hh in range(4)], axis=1)
    r = lax.broadcasted_iota(jnp.int32, (CHUNK, gw), 0)
    lane = lax.broadcasted_iota(jnp.int32, (CHUNK, gw), 1)
    sp = lane % S_HEAD_DIM
    mask = (sp <= r) if d == 0 else (sp >= r)
    seg = jnp.where(mask, jnp.exp(s_l - srow), 0.0)
    m_all = _dot(cg, bt4) * seg
    rr = lax.broadcasted_iota(jnp.int32, (gw, gw), 0) // S_HEAD_DIM
    ll = lax.broadcasted_iota(jnp.int32, (gw, gw), 1) // S_HEAD_DIM
    xd = jnp.where(rr == ll, jnp.concatenate([xdt] * 4, axis=0), 0.0)
    return _dot(m_all, xd) + jnp.exp(s_l) * _dot(cg, st_old)


def _ssd_kernel(emit_ctx,
                x_ref, bc_ref, z_ref, g_ref, xc_ref, bcc_ref, zc_ref, gc_ref,
                cwx_ref, cwb_ref, cbx_ref, cbb_ref, gb_ref, al_ref, dsk_ref, ng_ref,
                y_ref, yc_ref,
                x_s, bc_s, xc_s, bcc_s, acc_s, accc_s, st_s):
    t_lat, t_ctx = x_ref.shape[1], xc_ref.shape[1]
    gw = 4 * S_HEAD_DIM

    for c in range(S_WIDTH // LANES):
        sl = slice(c * LANES, (c + 1) * LANES)
        for src, dst, cw, cb in ((x_ref, x_s, cwx_ref, cbx_ref), (bc_ref, bc_s, cwb_ref, cbb_ref),
                                 (xc_ref, xc_s, cwx_ref, cbx_ref), (bcc_ref, bcc_s, cwb_ref, cbb_ref)):
            dst[:, sl] = _silu(_dwconv3(src[0, :, sl].astype(F32), cw[:, sl], cb[:, sl])).astype(dst.dtype)

    acc_s[...] = jnp.zeros_like(acc_s)
    accc_s[...] = jnp.zeros_like(accc_s)
    st_s[...] = jnp.zeros_like(st_s)
    acoef = -jnp.exp(al_ref[...])

    def run(nchunks, emit, xs, bcs, gr, acc):
        def body(j, carry):
            for d in range(2):
                c = j if d == 0 else nchunks - 1 - j
                rows = pl.ds(pl.multiple_of(c * CHUNK, CHUNK), CHUNK)
                dtt = _softplus(gr[0, rows, :] + gb_ref[...])
                st = _cumsum_rows(dtt * acoef, reverse=(d == 1))
                stt = st.T
                for g in range(S_GROUPS):
                    out = _ssd_stream(d, g, emit, xs[rows, g * gw:(g + 1) * gw],
                                      bcs[rows, g * S_STATE:(g + 1) * S_STATE],
                                      bcs[rows, (S_GROUPS + g) * S_STATE:(S_GROUPS + g + 1) * S_STATE],
                                      dtt, st, stt, st_s)
                    if emit:
                        acc[rows, g * gw:(g + 1) * gw] += out
            return carry
        lax.fori_loop(0, nchunks, body, 0, unroll=8 if nchunks >= 16 else 2)

    run(t_ctx // CHUNK, emit_ctx, xc_s, bcc_s, gc_ref, accc_s)
    run(t_lat // CHUNK, True, x_s, bc_s, g_ref, acc_s)

    def finish(acc, xs, zref, yref):
        for g in range(S_GROUPS):
            sl = slice(g * gw, (g + 1) * gw)
            yv = (acc[:, sl] + dsk_ref[:, sl] * xs[:, sl]) * _silu(zref[0, :, sl].astype(F32))
            yref[0, :, sl] = _rms(yv, ng_ref[:, sl]).astype(yref.dtype)

    finish(acc_s, x_s, z_ref, y_ref)
    if emit_ctx:
        finish(accc_s, xc_s, zc_ref, yc_ref)
    else:
        yc_ref[...] = jnp.zeros_like(yc_ref)


def _ssd(u, ug, uc, ugc, conv_w, conv_b, gate_bias, alog_row, dskip_row, norm_g, emit_ctx):
    b, t, _ = u.shape
    tc = uc.shape[1]
    w = S_WIDTH

    def ublk(tt, col):
        return pl.BlockSpec((1, tt, w), lambda bi, col=col: (bi, 0, col))

    def full(shape):
        return pl.BlockSpec(shape, lambda bi: (0,) * len(shape))

    return pl.pallas_call(
        functools.partial(_ssd_kernel, emit_ctx),
        grid=(b,),
        in_specs=[ublk(t, 8), ublk(t, 9), ublk(t, 7), pl.BlockSpec((1, t, LANES), lambda bi: (bi, 0, 0)),
                  ublk(tc, 8), ublk(tc, 9), ublk(tc, 7), pl.BlockSpec((1, tc, LANES), lambda bi: (bi, 0, 0)),
                  pl.BlockSpec((3, w), lambda bi: (0, 0)), pl.BlockSpec((3, w), lambda bi: (0, 1)),
                  pl.BlockSpec((1, w), lambda bi: (0, 0)), pl.BlockSpec((1, w), lambda bi: (0, 1)),
                  full((1, LANES)), full((1, LANES)), full((1, w)), full((1, w))],
        out_specs=[pl.BlockSpec((1, t, w), lambda bi: (bi, 0, 0)),
                   pl.BlockSpec((1, tc, w), lambda bi: (bi, 0, 0))],
        out_shape=[jax.ShapeDtypeStruct((b, t, w), BF16), jax.ShapeDtypeStruct((b, tc, w), BF16)],
        scratch_shapes=[pltpu.VMEM((t, w), F32), pltpu.VMEM((t, w), F32),
                        pltpu.VMEM((tc, w), F32), pltpu.VMEM((tc, w), F32),
                        pltpu.VMEM((t, w), F32), pltpu.VMEM((tc, w), F32),
                        pltpu.VMEM((2, S_GROUPS, S_STATE, 4 * S_HEAD_DIM), F32)],
        compiler_params=_cparams(("parallel",)),
        name="ssd",
    )(u, u, u, ug, uc, uc, uc, ugc, conv_w, conv_w, conv_b, conv_b, gate_bias, alog_row, dskip_row, norm_g)


def _qk_prep(x, g, cos, sin):
    lane = lax.broadcasted_iota(jnp.int32, x.shape, 1)
    lo = lane < A_SUB_DIM
    x2 = x * x
    s_lo = jnp.sum(jnp.where(lo, x2, 0.0), axis=-1, keepdims=True)
    s_hi = jnp.sum(jnp.where(lo, 0.0, x2), axis=-1, keepdims=True)
    r_lo = lax.rsqrt(s_lo * (1.0 / A_SUB_DIM) + NORM_EPS)
    r_hi = lax.rsqrt(s_hi * (1.0 / A_SUB_DIM) + NORM_EPS)
    y = x * jnp.where(lo, r_lo, r_hi) * g
    if cos is not None:
        swapped = jnp.where(lane % (2 * ROPE_FREQS) < ROPE_FREQS,
                            pltpu.roll(y, LANES - ROPE_FREQS, 1), pltpu.roll(y, ROPE_FREQS, 1))
        y = y * cos + swapped * sin
    return y


def _attn_kernel(lam_init, rope_q, src_rope, tq, *refs):
    n_src = len(src_rope)
    q_ref = refs[0]
    kv_refs = refs[1:1 + 2 * n_src]
    cos_ref, sin_ref, qg_ref, kg_ref, lam_ref, sg_ref = refs[1 + 2 * n_src:7 + 2 * n_src]
    o_ref = refs[7 + 2 * n_src]
    k_s, v_s = refs[8 + 2 * n_src:]
    qi = pl.program_id(1)

    @pl.when(qi == 0)
    def _():
        off = 0
        for si in range(n_src):
            k_ref, v_ref = kv_refs[2 * si], kv_refs[2 * si + 1]
            ts = k_ref.shape[1]
            for h in range(A_HEADS):
                sl = slice(h * LANES, (h + 1) * LANES)
                cs = (cos_ref[0:ts, :], sin_ref[0:ts, :]) if src_rope[si] else (None, None)
                k_s[h, off:off + ts, :] = _qk_prep(k_ref[0, :, sl].astype(F32), kg_ref[...], *cs).astype(BF16)
                v_s[h, off:off + ts, 0:LANES] = v_ref[0, :, sl]
                v_s[h, off:off + ts, LANES:2 * LANES] = jnp.ones((ts, LANES), BF16)
            off += ts

    lp = lam_ref[...]
    lam = (jnp.exp(jnp.sum(lp[0:1] * lp[1:2], axis=-1, keepdims=True))
           - jnp.exp(jnp.sum(lp[2:3] * lp[3:4], axis=-1, keepdims=True)) + lam_init)
    if rope_q:
        rows = pl.ds(pl.multiple_of(qi * tq, tq), tq)
        cs = (cos_ref[rows, :], sin_ref[rows, :])
    else:
        cs = (None, None)
    for h in range(A_HEADS):
        sl = slice(h * LANES, (h + 1) * LANES)
        qh = _qk_prep(q_ref[0, :, sl].astype(F32), qg_ref[...], *cs) * (A_SUB_DIM ** -0.5 * LOG2E)
        lo = lax.broadcasted_iota(jnp.int32, qh.shape, 1) < A_SUB_DIM
        q2 = jnp.concatenate([jnp.where(lo, qh, 0.0), jnp.where(lo, 0.0, qh)], axis=0).astype(BF16)
        s = _dot_nt(q2, k_s[h])
        e = jnp.exp2(s - jnp.max(s, axis=-1, keepdims=True)).astype(BF16)
        pv = jnp.dot(e, v_s[h], preferred_element_type=F32)
        o = (pv[0:tq, 0:LANES] / pv[0:tq, LANES:2 * LANES]
             - lam * (pv[tq:2 * tq, 0:LANES] / pv[tq:2 * tq, LANES:2 * LANES]))
        o_ref[0, :, sl] = (_rms(o, sg_ref[...]) * (1.0 - lam_init)).astype(o_ref.dtype)


def _attention(uq, sources, cos, sin, qg, kg, lam_p, sub_g, lam_init, rope_q):
    b, t, _ = uq.shape
    tq = min(t, 256)
    w = A_WIDTH
    tk = sum(s[0].shape[1] for s in sources)
    in_specs = [pl.BlockSpec((1, tq, w), lambda bi, qi: (bi, qi, 4))]
    args = [uq]
    for arr, _ in sources:
        ts = arr.shape[1]
        in_specs += [pl.BlockSpec((1, ts, w), lambda bi, qi: (bi, 0, 5)),
                     pl.BlockSpec((1, ts, w), lambda bi, qi: (bi, 0, 6))]
        args += [arr, arr]
    tcs = cos.shape[0]
    for shape in ((tcs, LANES), (tcs, LANES), (1, LANES), (1, LANES), (4, A_SUB_DIM), (1, LANES)):
        in_specs.append(pl.BlockSpec(shape, lambda bi, qi: (0, 0)))
    args += [cos, sin, qg, kg, lam_p, sub_g]
    return pl.pallas_call(
        functools.partial(_attn_kernel, lam_init, rope_q, tuple(r for _, r in sources), tq),
        grid=(b, t // tq),
        in_specs=in_specs,
        out_specs=pl.BlockSpec((1, tq, w), lambda bi, qi: (bi, qi, 0)),
        out_shape=jax.ShapeDtypeStruct((b, t, w), BF16),
        scratch_shapes=[pltpu.VMEM((A_HEADS, tk, LANES), BF16), pltpu.VMEM((A_HEADS, tk, 2 * LANES), BF16)],
        compiler_params=_cparams(("parallel", "arbitrary")),
        name="diff_attention",
    )(*args)


def _merge_kernel(x_ref, ym_ref, ya_ref, ys_ref, gm_ref, ga_ref, gs_ref, m2_ref,
                  wm_ref, wa_ref, ws_ref, wo_ref, o_ref):
    t = (_sigmoid(gm_ref[0].astype(F32)) * _dot(ym_ref[0], wm_ref[...])
         + _sigmoid(ga_ref[0].astype(F32)) * _dot(ya_ref[0], wa_ref[...])
         + _sigmoid(gs_ref[0].astype(F32)) * _dot(ys_ref[0], ws_ref[...]))
    o_ref[0] = x_ref[0] + m2_ref[0] * _dot(t, wo_ref[...])


def _merge(x, y_m, y_a, y_s, u, mod2, w_m, w_a, w_s, w_o):
    b, t, _ = x.shape
    tm = min(t, 1024)
    per_batch = mod2.shape[0] > 1
    mod_map = (lambda bi, i: (bi, 0, 0)) if per_batch else (lambda bi, i: (0, 0, 0))

    def yblk():
        return pl.BlockSpec((1, tm, 512), lambda bi, i: (bi, i, 0))

    def gblk(col):
        return pl.BlockSpec((1, tm, D_MODEL), lambda bi, i, col=col: (bi, i, col))

    def wblk(k):
        return pl.BlockSpec((k, D_MODEL), lambda bi, i: (0, 0), pipeline_mode=pl.Buffered(1))

    return pl.pallas_call(
        _merge_kernel,
        grid=(b, t // tm),
        in_specs=[pl.BlockSpec((1, tm, D_MODEL), lambda bi, i: (bi, i, 0)), yblk(), yblk(), yblk(),
                  gblk(5), gblk(6), gblk(7), pl.BlockSpec((1, 1, D_MODEL), mod_map),
                  wblk(512), wblk(512), wblk(512), wblk(D_MODEL)],
        out_specs=pl.BlockSpec((1, tm, D_MODEL), lambda bi, i: (bi, i, 0)),
        out_shape=jax.ShapeDtypeStruct(x.shape, F32),
        compiler_params=_cparams(("parallel", "parallel")),
        name="merge",
    )(x, y_m, y_a, y_s, u, u, u, mod2, w_m, w_a, w_s, w_o)


def _ffn_kernel(seq, x_ref, sh_ref, sc_ref, gt_ref, g_ref, wa_ref, wg_ref, cwa_ref, cwg_ref, wd_ref, o_ref, h_s):
    j = pl.program_id(1)
    last = pl.num_programs(1) - 1

    @pl.when(j == 0)
    def _():
        h = _rms(x_ref[0], g_ref[...]) * (1.0 + sc_ref[0]) + sh_ref[0]
        h_s[...] = h.astype(BF16)
        o_ref[...] = jnp.zeros_like(o_ref)

    a = _dwconv3(jnp.dot(h_s[...], wa_ref[...], preferred_element_type=F32), cwa_ref[...], period=seq)
    g = _dwconv3(jnp.dot(h_s[...], wg_ref[...], preferred_element_type=F32), cwg_ref[...], period=seq)
    o_ref[0] += _dot(_silu(g) * a, wd_ref[...])

    @pl.when(j == last)
    def _():
        o_ref[0] = x_ref[0] + gt_ref[0] * o_ref[0]


def _ffn(x, shift, scale, gate, g, w_up, conv_w, w_down, seq=None):
    b, t, _ = x.shape
    tf = 256
    nf = D_FF // tf
    per_batch = shift.shape[0] > 1
    mod_map = (lambda bi, j: (bi, 0, 0)) if per_batch else (lambda bi, j: (0, 0, 0))
    mod_spec = pl.BlockSpec((1, 1, D_MODEL), mod_map)
    return pl.pallas_call(
        functools.partial(_ffn_kernel, t if seq is None else seq),
        grid=(b, nf),
        in_specs=[pl.BlockSpec((1, t, D_MODEL), lambda bi, j: (bi, 0, 0)), mod_spec, mod_spec, mod_spec,
                  pl.BlockSpec((1, D_MODEL), lambda bi, j: (0, 0)),
                  pl.BlockSpec((D_MODEL, tf), lambda bi, j: (0, j)),
                  pl.BlockSpec((D_MODEL, tf), lambda bi, j: (0, nf + j)),
                  pl.BlockSpec((3, tf), lambda bi, j: (0, j)),
                  pl.BlockSpec((3, tf), lambda bi, j: (0, nf + j)),
                  pl.BlockSpec((tf, D_MODEL), lambda bi, j: (j, 0))],
        out_specs=pl.BlockSpec((1, t, D_MODEL), lambda bi, j: (bi, 0, 0)),
        out_shape=jax.ShapeDtypeStruct(x.shape, F32),
        scratch_shapes=[pltpu.VMEM((t, D_MODEL), BF16)],
        compiler_params=_cparams(("parallel", "arbitrary")),
        name="conv_ffn",
    )(x, shift, scale, gate, g, w_up, w_up, conv_w, conv_w, w_down)


def _rope_tables(t):
    pos = jnp.arange(t)
    inv_freq = ROPE_BASE ** (-jnp.arange(ROPE_FREQS, dtype=F32) / ROPE_FREQS)
    ang = jnp.stack([pos // GRID_W, pos % GRID_W], axis=-1).astype(F32)[..., None] * inv_freq
    cos, sin = jnp.cos(ang), jnp.sin(ang)
    cos64 = jnp.concatenate([cos[:, 0], cos[:, 0], cos[:, 1], cos[:, 1]], axis=-1)
    sin64 = jnp.concatenate([-sin[:, 0], sin[:, 0], -sin[:, 1], sin[:, 1]], axis=-1)
    return jnp.tile(cos64, (1, 2)), jnp.tile(sin64, (1, 2))


def _gate_lanes():
    idx = [-1] * GATE_W
    for q in range(4):
        for d in range(2):
            for h in range(M_HEADS):
                idx[q * 8 + d * 4 + h] = d * M_HEADS + h
                idx[LANES + q * 8 + d * 4 + h] = 2 * M_HEADS + d * M_HEADS + h
    for d in range(2):
        for h in range(S_HEADS):
            idx[DT_LANE0 + d * S_HEADS + h] = 4 * M_HEADS + d * S_HEADS + h
    return idx


def _gate_row(i_vals, f_vals, dt_vals):
    idx = jnp.array(_gate_lanes())
    src = jnp.concatenate([i_vals.reshape(-1), f_vals.reshape(-1), dt_vals.reshape(-1)])
    return jnp.where(idx >= 0, src[jnp.maximum(idx, 0)], 0.0)[None, :]


def _gate_weights(wl):
    idx = jnp.array(_gate_lanes())
    mg = wl[:, _MG0:_MG1].reshape(D_MODEL, 2, 2, M_HEADS)
    src = jnp.concatenate([mg[:, :, 0].reshape(D_MODEL, -1), mg[:, :, 1].reshape(D_MODEL, -1),
                           wl[:, _DT0:_DT1]], axis=1)
    return jnp.where(idx >= 0, src[:, jnp.maximum(idx, 0)], 0.0)


def kernel(x, c, ctx, c_ctx, w_mod, b_mod, norm1_g, norm2_g, w_in, m_conv_w, m_conv_b, m_igate_b, m_fgate_b, m_norm_g, a_qnorm_g, a_knorm_g, a_lambda, a_subln_g, s_conv_w, s_conv_b, s_dt_bias, s_a_log, s_d, s_norm_g, w_branch_m, w_branch_a, w_branch_s, w_out, w_up, ffn_conv_w, w_down):
    b, t, _ = x.shape
    cos, sin = _rope_tables(t)

    c_all = jnp.concatenate([c, c_ctx[None, :], jnp.zeros((7, D_MODEL), F32)], axis=0)
    mod_all = _modulation(c_all, w_mod, b_mod)

    xc = ctx
    for l in range(DEPTH):
        emit = l < DEPTH - 1
        lam_init = 0.8 - 0.6 * math.exp(-0.3 * l)
        mod = [mod_all[l, :b, i * D_MODEL:(i + 1) * D_MODEL][:, None, :] for i in range(N_MOD)]
        modc = [mod_all[l, b:b + 1, i * D_MODEL:(i + 1) * D_MODEL][:, None, :] for i in range(N_MOD)]

        wl = w_in[l]
        w_main = jnp.concatenate([wl[:, :_MG0], wl[:, _MG1:_DT0], wl[:, _DT1:]], axis=1).astype(BF16)
        w_small = _gate_weights(wl).astype(BF16)
        zeros8 = jnp.zeros((2 * M_HEADS,), F32)
        gate_bias = _gate_row(m_igate_b[l], m_fgate_b[l], s_dt_bias[l])
        alog_row = _gate_row(zeros8, zeros8, s_a_log[l])[:, :LANES]
        dskip_row = jnp.repeat(s_d[l], S_HEAD_DIM)[None, :]
        qg = jnp.tile(a_qnorm_g[l], 2)[None, :]
        kg = jnp.tile(a_knorm_g[l], 2)[None, :]

        n1 = norm1_g[l][None, :]
        u, ug = _inproj(x, mod[0], mod[1], n1, w_main, w_small)
        tc = xc.shape[1]
        uc, ugc = _inproj(xc.reshape(1, b * tc, D_MODEL), modc[0], modc[1], n1, w_main, w_small)
        uc, ugc = uc.reshape(b, tc, U_MAIN), ugc.reshape(b, tc, GATE_W)

        y_m, y_mc = _mlstm(u, ug, uc, ugc, m_conv_w[l], m_conv_b[l][None, :], gate_bias,
                           m_norm_g[l][None, :], emit)
        y_s, y_sc = _ssd(u, ug, uc, ugc, s_conv_w[l], s_conv_b[l][None, :], gate_bias, alog_row,
                         dskip_row, s_norm_g[l][None, :], emit)
        y_a = _attention(u, [(u, True), (uc, False)], cos, sin, qg, kg, a_lambda[l],
                         a_subln_g[l][None, :], lam_init, True)

        wm, wa, ws, wo = (w_branch_m[l].astype(BF16), w_branch_a[l].astype(BF16),
                          w_branch_s[l].astype(BF16), w_out[l].astype(BF16))
        wu, wd = w_up[l].astype(BF16), w_down[l].astype(BF16)
        n2 = norm2_g[l][None, :]
        x = _merge(x, y_m, y_a, y_s, u, mod[2], wm, wa, ws, wo)
        x = _ffn(x, mod[3], mod[4], mod[5], n2, wu, ffn_conv_w[l], wd)

        if emit:
            y_ac = _attention(uc, [(uc, False)], cos, sin, qg, kg, a_lambda[l],
                              a_subln_g[l][None, :], lam_init, False)
            flat = lambda a: a.reshape(1, b * tc, a.shape[-1])
            xc = _merge(flat(xc), flat(y_mc), flat(y_ac), flat(y_sc), flat(uc), modc[2], wm, wa, ws, wo)
            pack = math.gcd(b, max(1, 2048 // tc))
            xc = _ffn(xc.reshape(b // pack, pack * tc, D_MODEL), modc[3], modc[4], modc[5], n2, wu,
                      ffn_conv_w[l], wd, seq=tc).reshape(b, tc, D_MODEL)
    return x
```
